```python
import math
import jax
import jax.numpy as jnp
from jax import lax
import numpy as np

D_MODEL = 1024
BATCH = 8
SEQ = 4096
DEPTH = 2
DEC_BATCH = 32
DEC_SEQ = 8
PAST_LEN = 16384
PAGE_SIZE = 128

RET_H = 4
RET_DK = 64
RET_DV = 128
RET_CHUNK = 128
RET_THETA = 10000.0
MOBA_H = 8
MOBA_HD = 64
MOBA_BLOCK = 256
MOBA_TOPK = 3
DSA_H = 8
DSA_KVH = 2
DSA_HD = 64
IDX_H = 8
IDX_D = 64
DSA_TOPK = 256
MEM_TOKENS = 256
MEM_H = 4
MEM_HD = 128
BR_W = 512
N_BRANCH = 4
ROPE_THETA = 500000.0
ROT_DIV = 4
Q_BLOCK = 128
EPS = 1e-6

IN_SIZES = (
    RET_H * RET_DK,
    RET_H * RET_DK,
    RET_H * RET_DV,
    BR_W,
    MOBA_H * MOBA_HD,
    MOBA_H * MOBA_HD,
    MOBA_H * MOBA_HD,
    BR_W,
    DSA_H * DSA_HD,
    DSA_KVH * DSA_HD,
    DSA_KVH * DSA_HD,
    BR_W,
    IDX_H * IDX_D,
    IDX_D,
    IDX_H,
    MEM_H * MEM_HD,
    BR_W,
    N_BRANCH * D_MODEL,
)
N_IN = sum(IN_SIZES)

kernel_name = 'hybrid_retention_moba_dsa_memory_step'


def rmsnorm(x, g):
    xf = x.astype(jnp.float32)
    y = xf * lax.rsqrt(jnp.mean(xf * xf, axis=-1, keepdims=True) + EPS)
    return (y * g.astype(jnp.float32)).astype(x.dtype)


def rope(x, pos, rot_dim, theta):
    half = rot_dim // 2
    inv = jnp.power(theta, -jnp.arange(half, dtype=jnp.float32) * 2.0 / rot_dim)
    ang = pos.astype(jnp.float32)[:, None] * inv[None, :]
    cos = jnp.cos(ang)[None, :, None, :]
    sin = jnp.sin(ang)[None, :, None, :]
    x1 = x[..., :half].astype(jnp.float32)
    x2 = x[..., half:rot_dim].astype(jnp.float32)
    rot = jnp.concatenate([x1 * cos - x2 * sin, x2 * cos + x1 * sin], axis=-1).astype(x.dtype)
    return jnp.concatenate([rot, x[..., rot_dim:]], axis=-1)


def project(x, pos, g_norm, w_in):
    B, T, _ = x.shape
    h = rmsnorm(x, g_norm)
    z = jnp.einsum('btd,dn->btn', h, w_in)
    cuts = [int(c) for c in np.cumsum(IN_SIZES)[:-1]]
    (rq, rk, rv, rg, mq, mk, mv, mg, dq, dk, dv, dg, iq, ik, iw, eq, eg, bg) = jnp.split(z, cuts, axis=-1)
    hd = lambda a, n: a.reshape(B, T, n, -1)
    return {
        'ret_q': rope(hd(rq, RET_H), pos, RET_DK, RET_THETA),
        'ret_k': rope(hd(rk, RET_H), pos, RET_DK, RET_THETA) * (RET_DK ** -0.5),
        'ret_v': hd(rv, RET_H),
        'ret_g': rg,
        'moba_q': rope(hd(mq, MOBA_H), pos, MOBA_HD // ROT_DIV, ROPE_THETA),
        'moba_k': rope(hd(mk, MOBA_H), pos, MOBA_HD // ROT_DIV, ROPE_THETA),
        'moba_v': hd(mv, MOBA_H),
        'moba_g': mg,
        'dsa_q': rope(hd(dq, DSA_H), pos, DSA_HD // ROT_DIV, ROPE_THETA),
        'dsa_k': rope(hd(dk, DSA_KVH), pos, DSA_HD // ROT_DIV, ROPE_THETA),
        'dsa_v': hd(dv, DSA_KVH),
        'dsa_g': dg,
        'idx_q': rope(hd(iq, IDX_H), pos, IDX_D // ROT_DIV, ROPE_THETA),
        'idx_k': rope(hd(ik, 1), pos, IDX_D // ROT_DIV, ROPE_THETA)[:, :, 0],
        'idx_w': iw,
        'mem_q': hd(eq, MEM_H),
        'mem_g': eg,
        'br_g': bg,
    }


def retention_chunk(q, k, v, S):
    q = q.astype(jnp.float32)
    k = k.astype(jnp.float32)
    v = v.astype(jnp.float32)
    S = S.astype(jnp.float32)
    C = q.shape[1]
    lg = jnp.log1p(-jnp.exp2(-5.0 - jnp.arange(RET_H, dtype=jnp.float32)))
    n = jnp.arange(C, dtype=jnp.float32)
    diff = n[:, None] - n[None, :]
    dec = jnp.where(diff[None] >= 0, jnp.exp(jnp.maximum(diff, 0.0)[None] * lg[:, None, None]), 0.0)
    inner = jnp.einsum('bihd,bjhd->bhij', q, k) * dec[None]
    o = jnp.einsum('bhij,bjhe->bihe', inner, v)
    o = o + jnp.einsum('bihd,bhde->bihe', q, S) * jnp.exp((n[:, None] + 1.0) * lg[None, :])[None, :, :, None]
    k_dec = jnp.exp((C - 1.0 - n)[:, None] * lg[None, :])
    S_new = jnp.exp(C * lg)[None, :, None, None] * S + jnp.einsum('bjhd,bjhe,jh->bhde', k, v, k_dec)
    return o, S_new


def retention_prompt(q, k, v):
    B, T, H, _ = q.shape
    nc = T // RET_CHUNK
    def to_chunks(a):
        return a.astype(jnp.float32).reshape(B, nc, RET_CHUNK, H, a.shape[-1]).transpose(1, 0, 2, 3, 4)
    S0 = jnp.zeros((B, H, RET_DK, RET_DV), jnp.float32)
    def step(S, qkv):
        o, S = retention_chunk(qkv[0], qkv[1], qkv[2], S)
        return S, o
    S, o = lax.scan(step, S0, (to_chunks(q), to_chunks(k), to_chunks(v)))
    return o.transpose(1, 0, 2, 3, 4).reshape(B, T, H, RET_DV), S


def headnorm(o, g):
    mu = jnp.mean(o, axis=-1, keepdims=True)
    var = jnp.mean(jnp.square(o - mu), axis=-1, keepdims=True)
    y = (o - mu) * lax.rsqrt(var + EPS)
    return y.reshape(o.shape[0], o.shape[1], -1) * g.astype(jnp.float32)


def map_query_blocks(fn, q_arrays, pos, B, T):
    qb = math.gcd(T, Q_BLOCK)
    nqb = T // qb
    xs = tuple(a.reshape((B * nqb, qb) + a.shape[2:]) for a in q_arrays)
    pos_b = jnp.broadcast_to(pos.reshape(1, nqb, qb), (B, nqb, qb)).reshape(B * nqb, qb)
    bidx = jnp.repeat(jnp.arange(B, dtype=jnp.int32), nqb)
    out = lax.map(fn, xs + (pos_b, bidx))
    return out.reshape((B, T) + out.shape[2:])


def moba_attend(q, pos, K, V):
    B, T, H, hd = q.shape
    L = K.shape[1]
    nb = -(-L // MOBA_BLOCK)
    lpad = nb * MOBA_BLOCK
    padw = ((0, 0), (0, lpad - L), (0, 0), (0, 0))
    Kp = jnp.pad(K, padw)
    Vp = jnp.pad(V, padw)
    kbar = Kp.astype(jnp.float32).reshape(B, nb, MOBA_BLOCK, H, hd).mean(axis=2)
    n_sel = min(MOBA_TOPK, nb)
    ns = n_sel * MOBA_BLOCK
    scale = hd ** -0.5
    blk = jnp.arange(MOBA_BLOCK, dtype=jnp.int32)
    hidx = jnp.arange(H)[None, :, None]
    blocks = jnp.arange(nb, dtype=jnp.int32)

    def one(args):
        qb, pb, bi = args
        Kb, Vb, kb = Kp[bi], Vp[bi], kbar[bi]
        nq = qb.shape[0]
        cur = pb // MOBA_BLOCK
        gate = jnp.einsum('qhd,nhd->qhn', qb.astype(jnp.float32), kb)
        gate = jnp.where(blocks[None, None, :] < cur[:, None, None], gate, -jnp.inf)
        _, sel = lax.top_k(gate, n_sel)
        sel_ok = jnp.repeat(sel < cur[:, None, None], MOBA_BLOCK, axis=-1)
        kidx = (sel[..., None] * MOBA_BLOCK + blk).reshape(nq, H, ns)
        Ks = jnp.swapaxes(Kb, 0, 1)[hidx, kidx]
        Vs = jnp.swapaxes(Vb, 0, 1)[hidx, kidx]
        own = cur[:, None] * MOBA_BLOCK + blk
        Ko, Vo = Kb[own], Vb[own]
        s_sel = jnp.where(sel_ok, jnp.einsum('qhd,qhnd->qhn', qb, Ks).astype(jnp.float32) * scale, -jnp.inf)
        s_own = jnp.where((own <= pb[:, None])[:, None, :],
                          jnp.einsum('qhd,qnhd->qhn', qb, Ko).astype(jnp.float32) * scale, -jnp.inf)
        p = jax.nn.softmax(jnp.concatenate([s_sel, s_own], axis=-1), axis=-1)
        o = (jnp.einsum('qhn,qhnd->qhd', p[..., :ns].astype(Vs.dtype), Vs)
             + jnp.einsum('qhn,qnhd->qhd', p[..., ns:].astype(Vo.dtype), Vo))
        return o.reshape(nq, H * hd)

    return map_query_blocks(one, (q,), pos, B, T)


def dsa_attend(q, qi, wi, pos, K, V, KI):
    B, T, HQ, hd = q.shape
    L, kvh = K.shape[1], K.shape[2]
    grp = HQ // kvh
    n_sel = min(DSA_TOPK, L // 4)
    keypos = jnp.arange(L, dtype=jnp.int32)

    def one(args):
        qb, qib, wb, pb, bi = args
        Kb, Vb, KIb = K[bi], V[bi], KI[bi]
        nq = qb.shape[0]
        rel = jax.nn.relu(jnp.einsum('qhd,sd->qhs', qib.astype(jnp.float32), KIb.astype(jnp.float32)) * (IDX_D ** -0.5))
        score = jnp.einsum('qhs,qh->qs', rel, wb.astype(jnp.float32) * (IDX_H ** -0.5))
        score = jnp.where(keypos[None, :] <= pb[:, None], score, -jnp.inf)
        _, sel = lax.top_k(score, n_sel)
        ok = sel <= pb[:, None]
        Ks, Vs = Kb[sel], Vb[sel]
        qg = qb.reshape(nq, kvh, grp, hd)
        s = jnp.einsum('qkgd,qnkd->qkgn', qg, Ks).astype(jnp.float32) * (hd ** -0.5)
        s = jnp.where(ok[:, None, None, :], s, -jnp.inf)
        p = jax.nn.softmax(s, axis=-1)
        o = jnp.einsum('qkgn,qnkd->qkgd', p.astype(Vs.dtype), Vs)
        return o.reshape(nq, HQ * hd)

    return map_query_blocks(one, (q, qi, wi), pos, B, T)


def mem_kv(mem, g_mem, w_mem_kv):
    B = mem.shape[0]
    kv = jnp.einsum('bsd,dn->bsn', rmsnorm(mem, g_mem), w_mem_kv)
    k, v = jnp.split(kv, 2, axis=-1)
    return k.reshape(B, -1, MEM_H, MEM_HD), v.reshape(B, -1, MEM_H, MEM_HD)


def mem_attend(q, mk, mv):
    B, T = q.shape[0], q.shape[1]
    s = jnp.einsum('bthd,bshd->bhts', q, mk).astype(jnp.float32) * (MEM_HD ** -0.5)
    p = jax.nn.softmax(s, axis=-1)
    o = jnp.einsum('bhts,bshd->bthd', p.astype(mv.dtype), mv)
    return o.reshape(B, T, MEM_H * MEM_HD)


def merge_branches(x, pr, o_ret, o_moba, o_dsa, o_mem, g_ret, w_branch, w_out):
    B, T, _ = x.shape
    silu = jax.nn.silu
    outs = jnp.stack([
        headnorm(o_ret, g_ret) * silu(pr['ret_g']),
        o_moba * silu(pr['moba_g']),
        o_dsa * silu(pr['dsa_g']),
        o_mem * silu(pr['mem_g']),
    ], axis=2)
    z = jnp.einsum('btnc,ncd->btnd', outs, w_branch)
    g = jax.nn.sigmoid(pr['br_g'].reshape(B, T, N_BRANCH, D_MODEL))
    merged = jnp.sum(g * z, axis=2)
    return x + jnp.einsum('btd,de->bte', merged, w_out).astype(x.dtype)


def gather_pages(pool, page_table):
    g = pool[page_table]
    return g.reshape((g.shape[0], g.shape[1] * g.shape[2]) + g.shape[3:])


def setup_inputs(seed: int = 0) -> dict:
    key = jax.random.key(seed)
    ks = jax.random.split(key, 24)
    f = jnp.float32
    n_pages = PAST_LEN // PAGE_SIZE
    n_pool = (DEC_BATCH * n_pages * 5) // 4
    nrm = lambda k, shp, s: jax.random.normal(k, shp, f) * s
    page_table = jax.random.permutation(ks[0], n_pool)[: DEC_BATCH * n_pages].reshape(DEC_BATCH, n_pages).astype(jnp.int32)
    return {
        'x_prompt': nrm(ks[1], (BATCH, SEQ, D_MODEL), 1.0),
        'x_sample': nrm(ks[2], (DEC_BATCH, DEC_SEQ, D_MODEL), 1.0),
        'cache_moba_k': nrm(ks[3], (DEPTH, n_pool, PAGE_SIZE, MOBA_H, MOBA_HD), 1.0),
        'cache_moba_v': nrm(ks[4], (DEPTH, n_pool, PAGE_SIZE, MOBA_H, MOBA_HD), 1.0),
        'cache_dsa_k': nrm(ks[5], (DEPTH, n_pool, PAGE_SIZE, DSA_KVH, DSA_HD), 1.0),
        'cache_dsa_v': nrm(ks[6], (DEPTH, n_pool, PAGE_SIZE, DSA_KVH, DSA_HD), 1.0),
        'cache_dsa_idx_k': nrm(ks[7], (DEPTH, n_pool, PAGE_SIZE, IDX_D), 1.0),
        'cache_mem_k': nrm(ks[8], (DEPTH, DEC_BATCH, MEM_TOKENS, MEM_H, MEM_HD), 1.0),
        'cache_mem_v': nrm(ks[9], (DEPTH, DEC_BATCH, MEM_TOKENS, MEM_H, MEM_HD), 1.0),
        'state_ret': nrm(ks[10], (DEPTH, DEC_BATCH, RET_H, RET_DK, RET_DV), 1.0),
        'page_table': page_table,
        'mem_prompt': nrm(ks[11], (BATCH, MEM_TOKENS, D_MODEL), 1.0),
        'g_norm': 1.0 + nrm(ks[12], (DEPTH, D_MODEL), 0.02),
        'w_in': nrm(ks[13], (DEPTH, D_MODEL, N_IN), D_MODEL ** -0.5),
        'g_ret': 1.0 + nrm(ks[14], (DEPTH, RET_H * RET_DV), 0.02),
        'g_mem': 1.0 + nrm(ks[15], (DEPTH, D_MODEL), 0.02),
        'w_mem_kv': nrm(ks[16], (DEPTH, D_MODEL, 2 * MEM_H * MEM_HD), D_MODEL ** -0.5),
        'w_branch': nrm(ks[17], (DEPTH, N_BRANCH, BR_W, D_MODEL), BR_W ** -0.5),
        'w_out': nrm(ks[18], (DEPTH, D_MODEL, D_MODEL), D_MODEL ** -0.5),
        'g_final': 1.0 + nrm(ks[19], (D_MODEL,), 0.02),
    }


def reference(x_prompt, x_sample, cache_moba_k, cache_moba_v, cache_dsa_k, cache_dsa_v, cache_dsa_idx_k,
              cache_mem_k, cache_mem_v, state_ret, page_table, mem_prompt, g_norm, w_in, g_ret, g_mem,
              w_mem_kv, w_branch, w_out, g_final):
    seq, dec_seq = x_prompt.shape[1], x_sample.shape[1]
    past_len = page_table.shape[1] * cache_moba_k.shape[2]
    pos_p = jnp.arange(seq, dtype=jnp.int32)
    pos_s = past_len + jnp.arange(dec_seq, dtype=jnp.int32)
    xp, xs = x_prompt, x_sample
    pmk, pmv, pdk, pdv, pdi, pret, pek, pev = [], [], [], [], [], [], [], []
    smk, smv, sdk, sdv, sdi, sret = [], [], [], [], [], []
    for l in range(DEPTH):
        pr = project(xp, pos_p, g_norm[l], w_in[l])
        o_ret, S_p = retention_prompt(pr['ret_q'], pr['ret_k'], pr['ret_v'])
        o_moba = moba_attend(pr['moba_q'], pos_p, pr['moba_k'], pr['moba_v'])
        o_dsa = dsa_attend(pr['dsa_q'], pr['idx_q'], pr['idx_w'], pos_p, pr['dsa_k'], pr['dsa_v'], pr['idx_k'])
        mk, mv = mem_kv(mem_prompt, g_mem[l], w_mem_kv[l])
        o_mem = mem_attend(pr['mem_q'], mk, mv)
        xp = merge_branches(xp, pr, o_ret, o_moba, o_dsa, o_mem, g_ret[l], w_branch[l], w_out[l])
        pmk.append(pr['moba_k']); pmv.append(pr['moba_v'])
        pdk.append(pr['dsa_k']); pdv.append(pr['dsa_v']); pdi.append(pr['idx_k'])
        pret.append(S_p); pek.append(mk); pev.append(mv)

        ps = project(xs, pos_s, g_norm[l], w_in[l])
        o_ret_s, S_s = retention_chunk(ps['ret_q'], ps['ret_k'], ps['ret_v'], state_ret[l])
        k_m = jnp.concatenate([gather_pages(cache_moba_k[l], page_table), ps['moba_k']], axis=1)
        v_m = jnp.concatenate([gather_pages(cache_moba_v[l], page_table), ps['moba_v']], axis=1)
        o_moba_s = moba_attend(ps['moba_q'], pos_s, k_m, v_m)
        k_d = jnp.concatenate([gather_pages(cache_dsa_k[l], page_table), ps['dsa_k']], axis=1)
        v_d = jnp.concatenate([gather_pages(cache_dsa_v[l], page_table), ps['dsa_v']], axis=1)
        k_i = jnp.concatenate([gather_pages(cache_dsa_idx_k[l], page_table), ps['idx_k']], axis=1)
        o_dsa_s = dsa_attend(ps['dsa_q'], ps['idx_q'], ps['idx_w'], pos_s, k_d, v_d, k_i)
        o_mem_s = mem_attend(ps['mem_q'], cache_mem_k[l], cache_mem_v[l])
        xs = merge_branches(xs, ps, o_ret_s, o_moba_s, o_dsa_s, o_mem_s, g_ret[l], w_branch[l], w_out[l])
        smk.append(ps['moba_k']); smv.append(ps['moba_v'])
        sdk.append(ps['dsa_k']); sdv.append(ps['dsa_v']); sdi.append(ps['idx_k'])
        sret.append(S_s)

    y_prompt = rmsnorm(xp, g_final)
    y_sample = rmsnorm(xs, g_final)
    return (y_prompt, y_sample,
            jnp.stack(pmk), jnp.stack(pmv), jnp.stack(pdk), jnp.stack(pdv), jnp.stack(pdi),
            jnp.stack(pret), jnp.stack(pek), jnp.stack(pev),
            jnp.stack(smk), jnp.stack(smv), jnp.stack(sdk), jnp.stack(sdv), jnp.stack(sdi),
            jnp.stack(sret))
```

```python
import functools
import math

import jax
import jax.numpy as jnp
import numpy as np
from jax import lax
from jax.experimental import pallas as pl
from jax.experimental.pallas import tpu as pltpu

F32 = jnp.float32
MXU_DTYPE = jnp.bfloat16

D_MODEL = 1024
PAGE = 128
RET_H, RET_DK, RET_DV = 4, 64, 128
RET_CHUNK = 128
RET_THETA = 10000.0
MOBA_H, MOBA_HD, MOBA_BLOCK, MOBA_TOPK = 8, 64, 256, 3
DSA_H, DSA_KVH, DSA_HD = 8, 2, 64
IDX_H, IDX_D = 8, 64
DSA_TOPK = 256
MEM_H, MEM_HD = 4, 128
BR_W = 512
N_BRANCH = 4
ROPE_THETA = 500000.0
ROT_DIV = 4
EPS = 1e-6
LANES = 128
NEG = -1e30
INT_MIN = -2147483648
VMEM_LIMIT = 56 * 1024 * 1024

W_RET = 2 * RET_H * RET_DK + RET_H * RET_DV + BR_W
W_MOBA = 3 * MOBA_H * MOBA_HD + BR_W
W_DSA = DSA_H * DSA_HD + 2 * DSA_KVH * DSA_HD + BR_W
W_IDX = IDX_H * IDX_D + IDX_D + IDX_H
W_IDX_PAD = 640
W_MEM = MEM_H * MEM_HD + BR_W
W_BG = N_BRANCH * D_MODEL


def _dot(a, b):
    return jnp.dot(a, b, preferred_element_type=F32)


def _dot_nt(a, b):
    return lax.dot_general(a, b, (((1,), (1,)), ((), ())), preferred_element_type=F32)


def _dot_tn(a, b):
    return lax.dot_general(a, b, (((0,), (0,)), ((), ())), preferred_element_type=F32)


def _mx(a):
    return a.astype(MXU_DTYPE)


def _silu(x):
    return x * (1.0 / (1.0 + jnp.exp(-x)))


def _sigmoid(x):
    return 1.0 / (1.0 + jnp.exp(-x))


def _rmsnorm(x, g):
    return x * lax.rsqrt(jnp.mean(x * x, axis=-1, keepdims=True) + EPS) * g


def _params(sem):
    return pltpu.CompilerParams(dimension_semantics=sem, vmem_limit_bytes=VMEM_LIMIT)


def _const_spec(shape):
    nd = len(shape)
    return pl.BlockSpec(shape, lambda *_: (0,) * nd)


def _rope_tables(pos, rot_dim, theta, hd):
    half = rot_dim // 2
    inv = jnp.power(theta, -jnp.arange(half, dtype=F32) * 2.0 / rot_dim)
    ang = pos.astype(F32)[:, None] * inv[None, :]
    cos, sin = jnp.cos(ang), jnp.sin(ang)
    t = pos.shape[0]
    z_half = jnp.zeros((t, half), F32)
    rest0 = jnp.zeros((t, hd - rot_dim), F32)
    rest1 = jnp.ones((t, hd - rot_dim), F32)
    c = jnp.concatenate([cos, cos, rest1], axis=1)
    sa = jnp.concatenate([-sin, z_half, rest0], axis=1)
    sb = jnp.concatenate([z_half, sin, rest0], axis=1)
    rep = LANES // hd
    return jnp.stack([jnp.tile(c, (1, rep)), jnp.tile(sa, (1, rep)), jnp.tile(sb, (1, rep))])


def _rope_apply(z, tab_ref, half):
    c, sa, sb = tab_ref[0], tab_ref[1], tab_ref[2]
    outs = []
    for s in range(z.shape[1] // LANES):
        zs = z[:, s * LANES:(s + 1) * LANES]
        outs.append(zs * c + pltpu.roll(zs, LANES - half, 1) * sa + pltpu.roll(zs, half, 1) * sb)
    return outs[0] if len(outs) == 1 else jnp.concatenate(outs, axis=1)


def _proj_kernel(x_ref, g_ref, wr_ref, wm_ref, wd_ref, wi_ref, we_ref, tr_ref, to_ref,
                 rq, rk, rv, rg, mq, mk, mv, mg, dq, dk, dv, dg, iq, ik, iw, eq, eg):
    h = _mx(_rmsnorm(x_ref[...], g_ref[...]))
    ret_half = RET_DK // 2
    oth_half = MOBA_HD // ROT_DIV // 2

    z = _dot(h, wr_ref[...])
    rq[...] = _rope_apply(z[:, 0:256], tr_ref, ret_half)
    rk[...] = _rope_apply(z[:, 256:512], tr_ref, ret_half) * (RET_DK ** -0.5)
    rv[...] = z[:, 512:1024]
    rg[...] = z[:, 1024:1536]

    z = _dot(h, wm_ref[...])
    mq[...] = _rope_apply(z[:, 0:512], to_ref, oth_half)
    mk[...] = _rope_apply(z[:, 512:1024], to_ref, oth_half)
    mv[...] = z[:, 1024:1536]
    mg[...] = z[:, 1536:2048]

    z = _dot(h, wd_ref[...])
    dq[...] = _rope_apply(z[:, 0:512], to_ref, oth_half)
    dk[...] = _rope_apply(z[:, 512:640], to_ref, oth_half)
    dv[...] = z[:, 640:768]
    dg[...] = z[:, 768:1280]

    z = _dot(h, wi_ref[...])
    iq[...] = _rope_apply(z[:, 0:512], to_ref, oth_half)
    tail = z[:, 512:640]
    ik[...] = _rope_apply(tail, to_ref, oth_half)[:, 0:IDX_D]
    iw[...] = tail[:, IDX_D:IDX_D + IDX_H]

    z = _dot(h, we_ref[...])
    eq[...] = z[:, 0:512]
    eg[...] = z[:, 512:1024]


def _project(x2, g, w, tab_ret, tab_oth, tm):
    n = x2.shape[0]
    nt = tab_ret.shape[1] // tm
    widths = dict(rq=256, rk=256, rv=512, rg=512, mq=512, mk=512, mv=512, mg=512,
                  dq=512, dk=128, dv=128, dg=512, iq=512, ik=64, iw=8, eq=512, eg=512)
    names = list(widths)
    tok = lambda wd: pl.BlockSpec((tm, wd), lambda i: (i, 0))
    tab = pl.BlockSpec((3, tm, LANES), lambda i: (0, i % nt, 0))
    outs = pl.pallas_call(
        _proj_kernel,
        grid=(n // tm,),
        in_specs=[tok(D_MODEL), _const_spec((1, D_MODEL)),
                  _const_spec((D_MODEL, W_RET)), _const_spec((D_MODEL, W_MOBA)),
                  _const_spec((D_MODEL, W_DSA)), _const_spec((D_MODEL, W_IDX_PAD)),
                  _const_spec((D_MODEL, W_MEM)), tab, tab],
        out_specs=[tok(widths[k]) for k in names],
        out_shape=[jax.ShapeDtypeStruct((n, widths[k]), F32) for k in names],
        compiler_params=_params(("parallel",)),
        name="project",
    )(x2, g, w["ret"], w["moba"], w["dsa"], w["idx"], w["mem"], tab_ret, tab_oth)
    return dict(zip(names, outs))


def _ret_kernel(q_ref, k_ref, v_ref, g_ref, s0_ref, dec_ref, qd_ref, kd_ref, cd_ref, gr_ref,
                o_ref, s_ref):
    @pl.when(pl.program_id(1) == 0)
    def _():
        s_ref[...] = s0_ref[...]

    q, k, v = q_ref[...], k_ref[...], v_ref[...]
    outs = []
    for h in range(RET_H):
        qh = _mx(q[:, h * RET_DK:(h + 1) * RET_DK])
        kh = k[:, h * RET_DK:(h + 1) * RET_DK]
        vh = _mx(v[:, h * RET_DV:(h + 1) * RET_DV])
        state = s_ref[0, h]
        inner = _dot_nt(qh, _mx(kh)) * dec_ref[h]
        o = _dot(_mx(inner), vh) + _dot(qh, _mx(state)) * qd_ref[h]
        s_ref[0, h] = cd_ref[h] * state + _dot_tn(_mx(kh * kd_ref[h]), vh)
        mu = jnp.mean(o, axis=-1, keepdims=True)
        d = o - mu
        var = jnp.mean(d * d, axis=-1, keepdims=True)
        outs.append(d * lax.rsqrt(var + EPS))
    o_ref[...] = jnp.concatenate(outs, axis=1) * gr_ref[...] * _silu(g_ref[...])


def _retention(pr, s0, g_ret, nb, t, chunk):
    nc = t // chunk
    lg = jnp.log1p(-jnp.exp2(-5.0 - jnp.arange(RET_H, dtype=F32)))
    n = jnp.arange(chunk, dtype=F32)
    diff = n[:, None] - n[None, :]
    dec = jnp.where(diff[None] >= 0, jnp.exp(jnp.maximum(diff, 0.0)[None] * lg[:, None, None]), 0.0)
    qd = jnp.exp((n[None, :] + 1.0) * lg[:, None])[:, :, None]
    kd = jnp.exp((chunk - 1.0 - n)[None, :] * lg[:, None])[:, :, None]
    cd = jnp.exp(chunk * lg)[:, None, None]
    tok = lambda wd: pl.BlockSpec((chunk, wd), lambda b, c: (b * nc + c, 0))
    st = pl.BlockSpec((1, RET_H, RET_DK, RET_DV), lambda b, c: (b, 0, 0, 0))
    return pl.pallas_call(
        _ret_kernel,
        grid=(nb, nc),
        in_specs=[tok(256), tok(256), tok(512), tok(512), st,
                  _const_spec((RET_H, chunk, chunk)), _const_spec((RET_H, chunk, 1)),
                  _const_spec((RET_H, chunk, 1)), _const_spec((RET_H, 1, 1)), _const_spec((1, 512))],
        out_specs=[tok(512), st],
        out_shape=[jax.ShapeDtypeStruct((nb * t, 512), F32),
                   jax.ShapeDtypeStruct((nb, RET_H, RET_DK, RET_DV), F32)],
        compiler_params=_params(("parallel", "arbitrary")),
        name="retention",
    )(pr["rq"], pr["rk"], pr["rv"], pr["rg"], s0, dec, qd, kd, cd, g_ret)


def _topk_rows_mask(gate, valid_f, col_f, k, n_cols):
    g = jnp.where(valid_f > 0, gate, -jnp.inf)
    sel = jnp.zeros_like(gate)
    for _ in range(k):
        mx = jnp.max(g, axis=-1, keepdims=True)
        idx = jnp.min(jnp.where(g == mx, col_f, float(n_cols)), axis=-1, keepdims=True)
        pick = col_f == idx
        sel = jnp.where(pick, 1.0, sel)
        g = jnp.where(pick, -jnp.inf, g)
    return sel * valid_f


def _sort_key(score):
    bits = lax.bitcast_convert_type(score, jnp.int32)
    key = bits ^ ((bits >> 31) & jnp.int32(0x7FFFFFFF))
    return jnp.where(score == 0.0, 0, key)


def _moba_p_kernel(q_ref, k_ref, v_ref, g_ref, o_ref, kbar_ref, *, nblk):
    qi = pl.program_id(2)
    tq = MOBA_BLOCK

    @pl.when(qi == 0)
    def _():
        kbar_ref[...] = jnp.mean(k_ref[...].reshape(nblk, MOBA_BLOCK, LANES), axis=1)

    q2 = q_ref[...]
    lane = lax.broadcasted_iota(jnp.int32, (tq, LANES), 1)
    blk_i = lax.broadcasted_iota(jnp.int32, (tq, nblk), 1)
    blk_f = blk_i.astype(F32)
    valid_f = jnp.where(blk_i < qi, 1.0, 0.0)
    row = lax.broadcasted_iota(jnp.int32, (tq, tq), 0)
    col = lax.broadcasted_iota(jnp.int32, (tq, tq), 1)
    kbar = _mx(kbar_ref[...])
    scale = MOBA_HD ** -0.5

    own0 = pl.multiple_of(qi * tq, tq)
    k_own = _mx(k_ref[pl.ds(own0, tq), :])
    v_own = _mx(v_ref[pl.ds(own0, tq), :])

    res = []
    for j in range(2):
        in_head = (lane >= MOBA_HD) if j else (lane < MOBA_HD)
        qm = _mx(jnp.where(in_head, q2, 0.0))
        gate = _dot_nt(qm, kbar)
        sel = _topk_rows_mask(gate, valid_f, blk_f, MOBA_TOPK, nblk)

        s = jnp.where(col <= row, _dot_nt(qm, k_own) * scale, NEG)
        m = jnp.max(s, axis=-1, keepdims=True)
        p = jnp.exp(s - m)
        l = jnp.sum(p, axis=-1, keepdims=True)
        acc = _dot(_mx(p), v_own)

        def body(n, carry, qm=qm, sel=sel):
            m, l, acc = carry
            k0 = pl.multiple_of(n * tq, tq)
            kb = _mx(k_ref[pl.ds(k0, tq), :])
            vb = _mx(v_ref[pl.ds(k0, tq), :])
            on = jnp.sum(jnp.where(blk_i == n, sel, 0.0), axis=-1, keepdims=True) > 0.0
            s = jnp.where(on, _dot_nt(qm, kb) * scale, NEG)
            m_new = jnp.maximum(m, jnp.max(s, axis=-1, keepdims=True))
            alpha = jnp.exp(m - m_new)
            p = jnp.exp(s - m_new)
            l = alpha * l + jnp.sum(p, axis=-1, keepdims=True)
            acc = alpha * acc + _dot(_mx(p), vb)
            return m_new, l, acc

        m, l, acc = lax.fori_loop(0, qi, body, (m, l, acc))
        res.append(acc / l)

    o = jnp.where(lane < MOBA_HD, res[0], res[1])
    o_ref[...] = o * _silu(g_ref[...])


def _moba_prompt(pr, nb, t):
    nblk = t // MOBA_BLOCK
    tq = MOBA_BLOCK
    nhp = MOBA_H * MOBA_HD // LANES
    qspec = pl.BlockSpec((tq, LANES), lambda b, hp, qi: (b * nblk + qi, hp))
    kspec = pl.BlockSpec((t, LANES), lambda b, hp, qi: (b, hp))
    return pl.pallas_call(
        functools.partial(_moba_p_kernel, nblk=nblk),
        grid=(nb, nhp, nblk),
        in_specs=[qspec, kspec, kspec, qspec],
        out_specs=qspec,
        out_shape=jax.ShapeDtypeStruct((nb * t, MOBA_H * MOBA_HD), F32),
        scratch_shapes=[pltpu.VMEM((nblk, LANES), F32)],
        compiler_params=_params(("parallel", "parallel", "arbitrary")),
        name="moba_prompt",
    )(pr["mq"], pr["mk"], pr["mv"], pr["mg"])


def _select_topk_bias(key_ref, bias_ref, nch, valid_fn, n_rows, n_cols, topk, idx_bits):
    col = lax.broadcasted_iota(jnp.int32, (n_rows, n_cols), 1)
    kf = float(topk)

    def count(pred_fn):
        def cb(c, acc):
            return acc + jnp.where(pred_fn(c, key_ref[c]), 1.0, 0.0)
        acc = lax.fori_loop(0, nch, cb, jnp.zeros((n_rows, n_cols), F32))
        return jnp.sum(acc, axis=-1, keepdims=True)

    def bit_body(i, thr):
        cand = thr + lax.shift_left(jnp.int32(1), 31 - i)
        cnt = count(lambda c, kc: kc >= cand)
        return jnp.where(cnt >= kf, cand, thr)

    thr = lax.fori_loop(0, 32, bit_body, jnp.full((n_rows, 1), INT_MIN, jnp.int32))
    need = kf - count(lambda c, kc: kc > thr)

    def idx_body(i, x):
        cand = x + lax.shift_left(jnp.int32(1), idx_bits - 1 - i)
        cnt = count(lambda c, kc: (kc == thr) & (c * n_cols + col < cand))
        return jnp.where(cnt < need, cand, x)

    jstar = lax.fori_loop(0, idx_bits, idx_body, jnp.zeros((n_rows, 1), jnp.int32)) + 1

    def wb(c, _):
        kc = key_ref[c]
        take = (kc > thr) | ((kc == thr) & (c * n_cols + col < jstar))
        bias_ref[c] = jnp.where(take & valid_fn(c), 0.0, NEG)
        return 0

    lax.fori_loop(0, nch, wb, 0)


def _dsa_p_kernel(q_ref, iq_ref, iw_ref, k_ref, v_ref, ki_ref, g_ref, o_ref, key_ref, bias_ref,
                  *, topk, idx_bits):
    qi = pl.program_id(1)
    tq = q_ref.shape[0]
    nch = qi + 1
    row = lax.broadcasted_iota(jnp.int32, (tq, tq), 0)
    col = lax.broadcasted_iota(jnp.int32, (tq, tq), 1)
    lane = lax.broadcasted_iota(jnp.int32, (tq, LANES), 1)

    def valid(c):
        return (col <= row) | (c < qi)

    iq = iq_ref[...]
    w = iw_ref[...] * (IDX_H ** -0.5)
    iqh = [_mx(iq[:, h * IDX_D:(h + 1) * IDX_D]) for h in range(IDX_H)]
    wh = [w[:, h:h + 1] for h in range(IDX_H)]

    def score_body(c, _):
        kic = _mx(ki_ref[pl.ds(pl.multiple_of(c * tq, tq), tq), :])
        sc = jnp.zeros((tq, tq), F32)
        for h in range(IDX_H):
            sc = sc + jnp.maximum(_dot_nt(iqh[h], kic) * (IDX_D ** -0.5), 0.0) * wh[h]
        key_ref[c] = _sort_key(jnp.where(valid(c), sc, -jnp.inf))
        return 0

    lax.fori_loop(0, nch, score_body, 0)
    _select_topk_bias(key_ref, bias_ref, nch, valid, tq, tq, topk, idx_bits)

    q = q_ref[...]
    g = g_ref[...]
    scale = DSA_HD ** -0.5
    grp = DSA_H // DSA_KVH
    for pair in range(DSA_H // 2):
        q2 = q[:, pair * LANES:(pair + 1) * LANES]
        kvh = (2 * pair) // grp
        in_kv = (lane >= DSA_HD) if kvh else (lane < DSA_HD)
        res = []
        for j in range(2):
            qg = q2 if j == kvh else pltpu.roll(q2, DSA_HD, 1)
            qm = _mx(jnp.where(in_kv, qg, 0.0))

            def body(c, carry, qm=qm):
                m, l, acc = carry
                k0 = pl.multiple_of(c * tq, tq)
                kb = _mx(k_ref[pl.ds(k0, tq), :])
                vb = _mx(v_ref[pl.ds(k0, tq), :])
                s = _dot_nt(qm, kb) * scale + bias_ref[c]
                m_new = jnp.maximum(m, jnp.max(s, axis=-1, keepdims=True))
                alpha = jnp.exp(m - m_new)
                p = jnp.exp(s - m_new)
                l = alpha * l + jnp.sum(p, axis=-1, keepdims=True)
                acc = alpha * acc + _dot(_mx(p), vb)
                return m_new, l, acc

            init = (jnp.full((tq, 1), NEG, F32), jnp.zeros((tq, 1), F32), jnp.zeros((tq, LANES), F32))
            m, l, acc = lax.fori_loop(0, nch, body, init)
            o = acc / l
            res.append(o if j == kvh else pltpu.roll(o, DSA_HD, 1))
        o2 = jnp.where(lane < DSA_HD, res[0], res[1])
        sl = slice(pair * LANES, (pair + 1) * LANES)
        o_ref[:, sl] = o2 * _silu(g[:, sl])


def _dsa_prompt(pr, nb, t):
    tq = 256
    nq = t // tq
    topk = min(DSA_TOPK, t // 4)
    idx_bits = max(1, math.ceil(math.log2(t)))
    tok = lambda wd: pl.BlockSpec((tq, wd), lambda b, qi: (b * nq + qi, 0))
    seq = lambda wd: pl.BlockSpec((t, wd), lambda b, qi: (b, 0))
    return pl.pallas_call(
        functools.partial(_dsa_p_kernel, topk=topk, idx_bits=idx_bits),
        grid=(nb, nq),
        in_specs=[tok(512), tok(512), tok(IDX_H), seq(LANES), seq(LANES), seq(IDX_D), tok(512)],
        out_specs=tok(512),
        out_shape=jax.ShapeDtypeStruct((nb * t, DSA_H * DSA_HD), F32),
        scratch_shapes=[pltpu.VMEM((nq, tq, tq), jnp.int32), pltpu.VMEM((nq, tq, tq), F32)],
        compiler_params=_params(("parallel", "arbitrary")),
        name="dsa_prompt",
    )(pr["dq"], pr["iq"], pr["iw"], pr["dk"], pr["dv"], pr["ik"], pr["dg"])


def _memkv_kernel(x_ref, g_ref, w_ref, k_ref, v_ref):
    h = _mx(_rmsnorm(x_ref[...], g_ref[...]))
    z = _dot(h, w_ref[...])
    half = MEM_H * MEM_HD
    k_ref[...] = z[:, :half]
    v_ref[...] = z[:, half:]


def _mem_kv(mem2, g, w):
    n = mem2.shape[0]
    tm = 256
    half = MEM_H * MEM_HD
    tok = lambda wd: pl.BlockSpec((tm, wd), lambda i: (i, 0))
    return pl.pallas_call(
        _memkv_kernel,
        grid=(n // tm,),
        in_specs=[tok(D_MODEL), _const_spec((1, D_MODEL)), _const_spec((D_MODEL, 2 * half))],
        out_specs=[tok(half), tok(half)],
        out_shape=[jax.ShapeDtypeStruct((n, half), F32)] * 2,
        compiler_params=_params(("parallel",)),
        name="mem_kv",
    )(mem2, g, w)


def _mem_kernel(q_ref, k_ref, v_ref, g_ref, o_ref):
    q, k, v = q_ref[...], k_ref[...], v_ref[...]
    outs = []
    for h in range(MEM_H):
        sl = slice(h * MEM_HD, (h + 1) * MEM_HD)
        s = _dot_nt(_mx(q[:, sl]), _mx(k[:, sl])) * (MEM_HD ** -0.5)
        m = jnp.max(s, axis=-1, keepdims=True)
        p = jnp.exp(s - m)
        p = p / jnp.sum(p, axis=-1, keepdims=True)
        outs.append(_dot(_mx(p), _mx(v[:, sl])))
    o_ref[...] = jnp.concatenate(outs, axis=1) * _silu(g_ref[...])


def _mem_attend(q, mk, mv, g, nb, t, tq):
    nq = t // tq
    ntok = mk.shape[0] // nb
    tok = pl.BlockSpec((tq, 512), lambda b, i: (b * nq + i, 0))
    kv = pl.BlockSpec((ntok, 512), lambda b, i: (b, 0))
    return pl.pallas_call(
        _mem_kernel,
        grid=(nb, nq),
        in_specs=[tok, kv, kv, tok],
        out_specs=tok,
        out_shape=jax.ShapeDtypeStruct((nb * t, 512), F32),
        compiler_params=_params(("parallel", "arbitrary")),
        name="mem_attend",
    )(q, mk, mv, g)


def _merge_kernel(x_ref, g_ref, wbg_ref, o0, o1, o2, o3, wb_ref, wo_ref, gf_ref, y_ref, *, final):
    x = x_ref[...]
    h = _mx(_rmsnorm(x, g_ref[...]))
    merged = jnp.zeros(x.shape, F32)
    for b, o in enumerate((o0, o1, o2, o3)):
        gate = _sigmoid(_dot(h, wbg_ref[:, b * D_MODEL:(b + 1) * D_MODEL]))
        merged = merged + gate * _dot(_mx(o[...]), wb_ref[b])
    y = x + _dot(_mx(merged), wo_ref[...])
    y_ref[...] = _rmsnorm(y, gf_ref[...]) if final else y


def _merge(x2, g, wbg, outs, wb, wo, gf, final, tm):
    n = x2.shape[0]
    tok = lambda wd: pl.BlockSpec((tm, wd), lambda i: (i, 0))
    return pl.pallas_call(
        functools.partial(_merge_kernel, final=final),
        grid=(n // tm,),
        in_specs=[tok(D_MODEL), _const_spec((1, D_MODEL)), _const_spec((D_MODEL, W_BG)),
                  tok(BR_W), tok(BR_W), tok(BR_W), tok(BR_W),
                  _const_spec((N_BRANCH, BR_W, D_MODEL)), _const_spec((D_MODEL, D_MODEL)),
                  _const_spec((1, D_MODEL))],
        out_specs=tok(D_MODEL),
        out_shape=jax.ShapeDtypeStruct((n, D_MODEL), F32),
        compiler_params=_params(("parallel",)),
        name="merge",
    )(x2, g, wbg, *outs, wb, wo, gf)


def _moba_s_kernel(pt_ref, qbd_ref, kn_ref, vn_ref, g_ref, *rest, pps, n_pages):
    kp = rest[:pps]
    vp = rest[pps:2 * pps]
    o_ref = rest[2 * pps]
    s_ref, gate_ref, bmax_ref, pown_ref, acc_ref = rest[2 * pps + 1:]
    ph, j = pl.program_id(1), pl.program_id(2)
    nblk = n_pages * PAGE // MOBA_BLOCK
    ppb = MOBA_BLOCK // PAGE
    nq = kn_ref.shape[0]
    scale = MOBA_HD ** -0.5

    @pl.when(ph == 0)
    def _():
        qbd = _mx(qbd_ref[0])
        for i in range(pps):
            s_ref[j * pps + i] = _dot(_mx(kp[i][...]), qbd)

    @pl.when((ph == 1) & (j == 0))
    def _():
        def stat(n, _):
            blk = s_ref[pl.ds(n * ppb, ppb)].reshape(MOBA_BLOCK, LANES)
            gate_ref[pl.ds(n, 1), :] = jnp.sum(blk, axis=0, keepdims=True) * (1.0 / MOBA_BLOCK)
            bmax_ref[pl.ds(n, 1), :] = jnp.max(blk, axis=0, keepdims=True)
            return 0
        lax.fori_loop(0, nblk, stat, 0)

        gate = gate_ref[...]
        blk_f = lax.broadcasted_iota(jnp.int32, (nblk, LANES), 0).astype(F32)
        sel = jnp.zeros((nblk, LANES), F32)
        gsel = gate
        for _ in range(min(MOBA_TOPK, nblk)):
            mx = jnp.max(gsel, axis=0, keepdims=True)
            idx = jnp.min(jnp.where(gsel == mx, blk_f, float(nblk)), axis=0, keepdims=True)
            pick = blk_f == idx
            sel = jnp.where(pick, 1.0, sel)
            gsel = jnp.where(pick, -jnp.inf, gsel)
        gate_ref[...] = sel

        lane = lax.broadcasted_iota(jnp.int32, (nq, LANES), 1)
        krow = lax.broadcasted_iota(jnp.int32, (nq, LANES), 0)
        s_own = _dot(_mx(kn_ref[...]), _mx(qbd_ref[0]))
        s_own = jnp.where(krow <= lane % nq, s_own, NEG)
        m = jnp.maximum(jnp.max(jnp.where(sel > 0, bmax_ref[...], NEG), axis=0, keepdims=True),
                        jnp.max(s_own, axis=0, keepdims=True))
        p_own = jnp.exp((s_own - m) * scale)

        def pbody(n, l):
            on = gate_ref[pl.ds(n, 1), :] > 0
            blk = s_ref[pl.ds(n * ppb, ppb)]
            p = jnp.where(on, jnp.exp((blk - m) * scale), 0.0)
            s_ref[pl.ds(n * ppb, ppb)] = p
            return l + jnp.sum(p.reshape(MOBA_BLOCK, LANES), axis=0, keepdims=True)
        l = lax.fori_loop(0, nblk, pbody, jnp.sum(p_own, axis=0, keepdims=True))
        bmax_ref[0:1, :] = 1.0 / l
        p_sq = jnp.concatenate([p_own / l, jnp.zeros((LANES - nq, LANES), F32)], axis=0)
        v_sq = jnp.concatenate([vn_ref[...], jnp.zeros((LANES - nq, vn_ref.shape[1]), F32)], axis=0)
        acc_ref[...] = _dot_tn(_mx(p_sq), _mx(v_sq))

    @pl.when(ph == 1)
    def _():
        rl = bmax_ref[0:1, :]
        acc = acc_ref[...]
        for i in range(pps):
            p = s_ref[j * pps + i] * rl
            acc = acc + _dot_tn(_mx(p), _mx(vp[i][...]))
        acc_ref[...] = acc

    @pl.when((ph == 1) & (j == pl.num_programs(2) - 1))
    def _():
        acc = acc_ref[...]
        parts = [acc[h * nq:(h + 1) * nq, h * MOBA_HD:(h + 1) * MOBA_HD] for h in range(MOBA_H)]
        o_ref[...] = jnp.concatenate(parts, axis=1) * _silu(g_ref[...])


def _moba_sample(ps, cache_k, cache_v, layer, page_table, nb, nq):
    n_pages = page_table.shape[1]
    pps = 8
    nj = n_pages // pps
    wd = MOBA_H * MOBA_HD
    q4 = ps["mq"].reshape(nb, nq, MOBA_H, MOBA_HD).transpose(0, 2, 3, 1)
    eye = jnp.eye(MOBA_H, dtype=F32)
    qbd = (q4[:, :, :, None, :] * eye[None, :, None, :, None]).reshape(nb, wd, MOBA_H * nq)
    qbd = jnp.pad(qbd, ((0, 0), (0, 0), (0, LANES - MOBA_H * nq)))
    ck = cache_k.reshape(cache_k.shape[0], cache_k.shape[1], PAGE, wd)
    cv = cache_v.reshape(cache_v.shape[0], cache_v.shape[1], PAGE, wd)

    def kmap(i):
        return lambda b, ph, j, pt: (layer, pt[b, jnp.where(ph == 0, j, nj - 1) * pps + i], 0, 0)

    def vmap_(i):
        return lambda b, ph, j, pt: (layer, pt[b, jnp.where(ph == 0, 0, j) * pps + i], 0, 0)

    page = lambda f: pl.BlockSpec((None, None, PAGE, wd), f)
    tok = pl.BlockSpec((nq, wd), lambda b, ph, j, pt: (b, 0))
    grid_spec = pltpu.PrefetchScalarGridSpec(
        num_scalar_prefetch=1,
        grid=(nb, 2, nj),
        in_specs=[pl.BlockSpec((1, wd, LANES), lambda b, ph, j, pt: (b, 0, 0)), tok, tok, tok]
                 + [page(kmap(i)) for i in range(pps)] + [page(vmap_(i)) for i in range(pps)],
        out_specs=tok,
        scratch_shapes=[pltpu.VMEM((n_pages, PAGE, LANES), F32),
                        pltpu.VMEM((n_pages * PAGE // MOBA_BLOCK, LANES), F32),
                        pltpu.VMEM((n_pages * PAGE // MOBA_BLOCK, LANES), F32),
                        pltpu.VMEM((8, LANES), F32),
                        pltpu.VMEM((LANES, wd), F32)],
    )
    return pl.pallas_call(
        functools.partial(_moba_s_kernel, pps=pps, n_pages=n_pages),
        grid_spec=grid_spec,
        out_shape=jax.ShapeDtypeStruct((nb * nq, wd), F32),
        compiler_params=_params(("parallel", "arbitrary", "arbitrary")),
        name="moba_sample",
    )(page_table, qbd, ps["mk"], ps["mv"], ps["mg"], *([ck] * pps), *([cv] * pps))


def _dsa_s_kernel(pt_ref, qi_ref, wi_ref, qz_ref, ikn_ref, kn_ref, vn_ref, g_ref, *rest,
                  pps, n_pages, topk, idx_bits):
    ip = rest[:pps]
    kp = rest[pps:2 * pps]
    vp = rest[2 * pps:3 * pps]
    o_ref = rest[3 * pps]
    key_ref, bias_ref, m_ref, l_ref, acc_ref = rest[3 * pps + 1:]
    ph, j = pl.program_id(1), pl.program_id(2)
    nq = ikn_ref.shape[0]
    nrow = IDX_H * nq
    scale = DSA_HD ** -0.5

    def idx_scores(ki):
        rel = jnp.maximum(_dot_nt(_mx(qi_ref[0]), _mx(ki)) * (IDX_D ** -0.5), 0.0) * wi_ref[0]
        return jnp.sum(rel.reshape(IDX_H, nq, ki.shape[0]), axis=0)

    @pl.when(ph == 0)
    def _():
        for i in range(pps):
            key_ref[j * pps + i] = _sort_key(idx_scores(ip[i][...]))

    def attend(s_bias, kb, vb):
        s = _dot_nt(_mx(qz_ref[0]), _mx(kb)) * scale + s_bias
        m = m_ref[...]
        m_new = jnp.maximum(m, jnp.max(s, axis=-1, keepdims=True))
        alpha = jnp.exp(m - m_new)
        p = jnp.exp(s - m_new)
        l_ref[...] = alpha * l_ref[...] + jnp.sum(p, axis=-1, keepdims=True)
        acc_ref[...] = alpha * acc_ref[...] + _dot(_mx(p), _mx(vb))
        m_ref[...] = m_new

    @pl.when((ph == 1) & (j == 0))
    def _():
        lane = lax.broadcasted_iota(jnp.int32, (nq, LANES), 1)
        qrow = lax.broadcasted_iota(jnp.int32, (nq, LANES), 0)
        new_valid = lane <= qrow
        ik_sq = jnp.concatenate([ikn_ref[...], jnp.zeros((LANES - nq, IDX_D), F32)], axis=0)
        key_ref[n_pages] = _sort_key(jnp.where(new_valid, idx_scores(ik_sq), -jnp.inf))

        def valid(c):
            return new_valid | (c < n_pages)
        _select_topk_bias(key_ref, bias_ref, n_pages + 1, valid, nq, LANES, topk, idx_bits)

        m_ref[...] = jnp.full(m_ref.shape, NEG, F32)
        l_ref[...] = jnp.zeros(l_ref.shape, F32)
        acc_ref[...] = jnp.zeros(acc_ref.shape, F32)
        k_sq = jnp.concatenate([kn_ref[...], jnp.zeros((LANES - nq, LANES), F32)], axis=0)
        v_sq = jnp.concatenate([vn_ref[...], jnp.zeros((LANES - nq, LANES), F32)], axis=0)
        attend(jnp.tile(bias_ref[n_pages], (DSA_H, 1)), k_sq, v_sq)

    @pl.when(ph == 1)
    def _():
        for i in range(pps):
            attend(jnp.tile(bias_ref[j * pps + i], (DSA_H, 1)), kp[i][...], vp[i][...])

    @pl.when((ph == 1) & (j == pl.num_programs(2) - 1))
    def _():
        o = acc_ref[...] / l_ref[...]
        grp = DSA_H // DSA_KVH
        parts = []
        for g in range(DSA_H):
            kvh = g // grp
            parts.append(o[g * nq:(g + 1) * nq, kvh * DSA_HD:(kvh + 1) * DSA_HD])
        o_ref[...] = jnp.concatenate(parts, axis=1) * _silu(g_ref[...])
    del nrow


def _dsa_sample(ps, cache_k, cache_v, cache_i, layer, page_table, nb, nq):
    n_pages = page_table.shape[1]
    pps = 8
    nj = n_pages // pps
    n_keys = n_pages * PAGE + nq
    topk = min(DSA_TOPK, n_keys // 4)
    idx_bits = max(1, math.ceil(math.log2((n_pages + 1) * PAGE)))
    grp = DSA_H // DSA_KVH
    qi = ps["iq"].reshape(nb, nq, IDX_H, IDX_D).transpose(0, 2, 1, 3).reshape(nb, IDX_H * nq, IDX_D)
    wi = (ps["iw"].reshape(nb, nq, IDX_H).transpose(0, 2, 1) * (IDX_H ** -0.5)).reshape(nb, IDX_H * nq, 1)
    q4 = ps["dq"].reshape(nb, nq, DSA_H, DSA_HD).transpose(0, 2, 1, 3)
    kv_of = (jnp.arange(DSA_H) // grp)[:, None] == jnp.arange(DSA_KVH)[None, :]
    qz = (q4[:, :, :, None, :] * kv_of.astype(F32)[None, :, None, :, None]).reshape(
        nb, DSA_H * nq, DSA_KVH * DSA_HD)
    ck = cache_k.reshape(cache_k.shape[0], cache_k.shape[1], PAGE, LANES)
    cv = cache_v.reshape(cache_v.shape[0], cache_v.shape[1], PAGE, LANES)

    def imap(i):
        return lambda b, ph, j, pt: (layer, pt[b, jnp.where(ph == 0, j, nj - 1) * pps + i], 0, 0)

    def kvmap(i):
        return lambda b, ph, j, pt: (layer, pt[b, jnp.where(ph == 0, 0, j) * pps + i], 0, 0)

    page = lambda wd, f: pl.BlockSpec((None, None, PAGE, wd), f)
    per_b = lambda r, c: pl.BlockSpec((1, r, c), lambda b, ph, j, pt: (b, 0, 0))
    tok = lambda wd: pl.BlockSpec((nq, wd), lambda b, ph, j, pt: (b, 0))
    grid_spec = pltpu.PrefetchScalarGridSpec(
        num_scalar_prefetch=1,
        grid=(nb, 2, nj),
        in_specs=[per_b(IDX_H * nq, IDX_D), per_b(IDX_H * nq, 1), per_b(DSA_H * nq, LANES),
                  tok(IDX_D), tok(LANES), tok(LANES), tok(512)]
                 + [page(IDX_D, imap(i)) for i in range(pps)]
                 + [page(LANES, kvmap(i)) for i in range(pps)]
                 + [page(LANES, kvmap(i)) for i in range(pps)],
        out_specs=tok(512),
        scratch_shapes=[pltpu.VMEM((n_pages + 1, nq, LANES), jnp.int32),
                        pltpu.VMEM((n_pages + 1, nq, LANES), F32),
                        pltpu.VMEM((DSA_H * nq, 1), F32),
                        pltpu.VMEM((DSA_H * nq, 1), F32),
                        pltpu.VMEM((DSA_H * nq, LANES), F32)],
    )
    return pl.pallas_call(
        functools.partial(_dsa_s_kernel, pps=pps, n_pages=n_pages, topk=topk, idx_bits=idx_bits),
        grid_spec=grid_spec,
        out_shape=jax.ShapeDtypeStruct((nb * nq, 512), F32),
        compiler_params=_params(("parallel", "arbitrary", "arbitrary")),
        name="dsa_sample",
    )(page_table, qi, wi, qz, ps["ik"], ps["dk"], ps["dv"], ps["dg"],
      *([cache_i] * pps), *([ck] * pps), *([cv] * pps))


def _layer_weights(w_in_l):
    cuts = np.cumsum([0, W_RET, W_MOBA, W_DSA, W_IDX, W_MEM, W_BG])
    seg = [w_in_l[:, int(a):int(b)] for a, b in zip(cuts[:-1], cuts[1:])]
    idx = jnp.pad(seg[3], ((0, 0), (0, W_IDX_PAD - W_IDX)))
    return dict(ret=_mx(seg[0]), moba=_mx(seg[1]), dsa=_mx(seg[2]), idx=_mx(idx),
                mem=_mx(seg[4]), bg=_mx(seg[5]))


def kernel(x_prompt, x_sample, cache_moba_k, cache_moba_v, cache_dsa_k, cache_dsa_v, cache_dsa_idx_k,
           cache_mem_k, cache_mem_v, state_ret, page_table, mem_prompt, g_norm, w_in, g_ret, g_mem,
           w_mem_kv, w_branch, w_out, g_final):
    nb_p, t_p, d = x_prompt.shape
    nb_s, t_s, _ = x_sample.shape
    depth = w_in.shape[0]
    past = page_table.shape[1] * PAGE
    pos_p = jnp.arange(t_p, dtype=jnp.int32)
    pos_s = past + jnp.arange(t_s, dtype=jnp.int32)
    tm_p = 256
    n_s = nb_s * t_s
    tab_p = (_rope_tables(pos_p, RET_DK, RET_THETA, RET_DK),
             _rope_tables(pos_p, MOBA_HD // ROT_DIV, ROPE_THETA, MOBA_HD))
    tab_s = tuple(jnp.tile(_rope_tables(pos_s, r, th, 64), (1, nb_s, 1))
                  for r, th in ((RET_DK, RET_THETA), (MOBA_HD // ROT_DIV, ROPE_THETA)))
    s0_p = jnp.zeros((nb_p, RET_H, RET_DK, RET_DV), F32)
    mem2 = mem_prompt.reshape(nb_p * mem_prompt.shape[1], d)
    gf = g_final.reshape(1, d)

    xp = x_prompt.reshape(nb_p * t_p, d)
    xs = x_sample.reshape(n_s, d)
    acc = {k: [] for k in ("pmk", "pmv", "pdk", "pdv", "pdi", "pret", "pek", "pev",
                           "smk", "smv", "sdk", "sdv", "sdi", "sret")}
    for l in range(depth):
        w = _layer_weights(w_in[l])
        gn = g_norm[l].reshape(1, d)
        gr = g_ret[l].reshape(1, RET_H * RET_DV)
        wb = _mx(w_branch[l])
        wo = _mx(w_out[l])
        final = l == depth - 1

        pr = _project(xp, gn, w, tab_p[0], tab_p[1], tm_p)
        o_ret, st_p = _retention(pr, s0_p, gr, nb_p, t_p, RET_CHUNK)
        o_moba = _moba_prompt(pr, nb_p, t_p)
        o_dsa = _dsa_prompt(pr, nb_p, t_p)
        mk, mv = _mem_kv(mem2, g_mem[l].reshape(1, d), _mx(w_mem_kv[l]))
        o_mem = _mem_attend(pr["eq"], mk, mv, pr["eg"], nb_p, t_p, 256)
        xp = _merge(xp, gn, w["bg"], (o_ret, o_moba, o_dsa, o_mem), wb, wo, gf, final, tm_p)
        acc["pmk"].append(pr["mk"]); acc["pmv"].append(pr["mv"])
        acc["pdk"].append(pr["dk"]); acc["pdv"].append(pr["dv"]); acc["pdi"].append(pr["ik"])
        acc["pret"].append(st_p); acc["pek"].append(mk); acc["pev"].append(mv)

        ps = _project(xs, gn, w, tab_s[0], tab_s[1], n_s)
        o_ret_s, st_s = _retention(ps, state_ret[l], gr, nb_s, t_s, t_s)
        o_moba_s = _moba_sample(ps, cache_moba_k, cache_moba_v, l, page_table, nb_s, t_s)
        o_dsa_s = _dsa_sample(ps, cache_dsa_k, cache_dsa_v, cache_dsa_idx_k, l, page_table, nb_s, t_s)
        mk_s = cache_mem_k[l].reshape(nb_s * cache_mem_k.shape[2], MEM_H * MEM_HD)
        mv_s = cache_mem_v[l].reshape(nb_s * cache_mem_v.shape[2], MEM_H * MEM_HD)
        o_mem_s = _mem_attend(ps["eq"], mk_s, mv_s, ps["eg"], nb_s, t_s, t_s)
        xs = _merge(xs, gn, w["bg"], (o_ret_s, o_moba_s, o_dsa_s, o_mem_s), wb, wo, gf, final, n_s)
        acc["smk"].append(ps["mk"]); acc["smv"].append(ps["mv"])
        acc["sdk"].append(ps["dk"]); acc["sdv"].append(ps["dv"]); acc["sdi"].append(ps["ik"])
        acc["sret"].append(st_s)

    st = lambda k, shape: jnp.stack(acc[k]).reshape((depth,) + shape)
    ntok = mem_prompt.shape[1]
    return (xp.reshape(nb_p, t_p, d), xs.reshape(nb_s, t_s, d),
            st("pmk", (nb_p, t_p, MOBA_H, MOBA_HD)), st("pmv", (nb_p, t_p, MOBA_H, MOBA_HD)),
            st("pdk", (nb_p, t_p, DSA_KVH, DSA_HD)), st("pdv", (nb_p, t_p, DSA_KVH, DSA_HD)),
            st("pdi", (nb_p, t_p, IDX_D)),
            st("pret", (nb_p, RET_H, RET_DK, RET_DV)),
            st("pek", (nb_p, ntok, MEM_H, MEM_HD)), st("pev", (nb_p, ntok, MEM_H, MEM_HD)),
            st("smk", (nb_s, t_s, MOBA_H, MOBA_HD)), st("smv", (nb_s, t_s, MOBA_H, MOBA_HD)),
            st("sdk", (nb_s, t_s, DSA_KVH, DSA_HD)), st("sdv", (nb_s, t_s, DSA_KVH, DSA_HD)),
            st("sdi", (nb_s, t_s, IDX_D)),
            st("sret", (nb_s, RET_H, RET_DK, RET_DV)))
```

```python
import functools
import math

import jax
import jax.numpy as jnp
import numpy as np
from jax import lax
from jax.experimental import pallas as pl
from jax.experimental.pallas import tpu as pltpu

F32 = jnp.float32
MXU_DTYPE = jnp.bfloat16

D_MODEL = 1024
PAGE = 128
RET_H, RET_DK, RET_DV = 4, 64, 128
RET_CHUNK = 128
RET_THETA = 10000.0
MOBA_H, MOBA_HD, MOBA_BLOCK, MOBA_TOPK = 8, 64, 256, 3
DSA_H, DSA_KVH, DSA_HD = 8, 2, 64
IDX_H, IDX_D = 8, 64
DSA_TOPK = 256
MEM_H, MEM_HD = 4, 128
BR_W = 512
N_BRANCH = 4
ROPE_THETA = 500000.0
ROT_DIV = 4
EPS = 1e-6
LANES = 128
HALF = LANES // 2
NEG = -1e30
INT_MIN = -2147483648
INT_MAX = 2147483647
LOG2E = math.log2(math.e)
VMEM_LIMIT = 56 * 1024 * 1024
SAMPLE_PAGES_PER_STEP = 16
N_STATES = 2

W_RET = 2 * RET_H * RET_DK + RET_H * RET_DV + BR_W
W_MOBA = 3 * MOBA_H * MOBA_HD + BR_W
W_DSA = DSA_H * DSA_HD + 2 * DSA_KVH * DSA_HD + BR_W
W_IDX = IDX_H * IDX_D + IDX_D + IDX_H
W_IDX_PAD = 640
W_MEM = MEM_H * MEM_HD + BR_W
W_BG = N_BRANCH * D_MODEL


def _dot(a, b):
    return jnp.dot(a, b, preferred_element_type=F32)


def _dot_nt(a, b):
    return lax.dot_general(a, b, (((1,), (1,)), ((), ())), preferred_element_type=F32)


def _dot_tn(a, b):
    return lax.dot_general(a, b, (((0,), (0,)), ((), ())), preferred_element_type=F32)


def _mx(a):
    return a.astype(MXU_DTYPE)


def _silu(x):
    return x * (1.0 / (1.0 + jnp.exp(-x)))


def _sigmoid(x):
    return 1.0 / (1.0 + jnp.exp(-x))


def _rmsnorm(x, g):
    return x * lax.rsqrt(jnp.mean(x * x, axis=-1, keepdims=True) + EPS) * g


def _params(sem):
    return pltpu.CompilerParams(dimension_semantics=sem, vmem_limit_bytes=VMEM_LIMIT)


def _const_spec(shape):
    nd = len(shape)
    return pl.BlockSpec(shape, lambda *_: (0,) * nd)


def _swap_halves(x):
    return pltpu.roll(x, HALF, 1)


def _rope_tables(pos, rot_dim, theta, hd):
    half = rot_dim // 2
    inv = jnp.power(theta, -jnp.arange(half, dtype=F32) * 2.0 / rot_dim)
    ang = pos.astype(F32)[:, None] * inv[None, :]
    cos, sin = jnp.cos(ang), jnp.sin(ang)
    t = pos.shape[0]
    z_half = jnp.zeros((t, half), F32)
    rest0 = jnp.zeros((t, hd - rot_dim), F32)
    rest1 = jnp.ones((t, hd - rot_dim), F32)
    c = jnp.concatenate([cos, cos, rest1], axis=1)
    sa = jnp.concatenate([-sin, z_half, rest0], axis=1)
    sb = jnp.concatenate([z_half, sin, rest0], axis=1)
    rep = LANES // hd
    return jnp.stack([jnp.tile(c, (1, rep)), jnp.tile(sa, (1, rep)), jnp.tile(sb, (1, rep))])


def _rope_apply(z, tab_ref, half):
    c, sa, sb = tab_ref[0], tab_ref[1], tab_ref[2]
    outs = []
    for s in range(z.shape[1] // LANES):
        zs = z[:, s * LANES:(s + 1) * LANES]
        outs.append(zs * c + pltpu.roll(zs, LANES - half, 1) * sa + pltpu.roll(zs, half, 1) * sb)
    return outs[0] if len(outs) == 1 else jnp.concatenate(outs, axis=1)


def _proj_kernel(x_ref, g_ref, wr_ref, wm_ref, wd_ref, wi_ref, we_ref, tr_ref, to_ref,
                 rq, rk, rv, rg, mq, mk, mv, mg, dq, dk, dv, dg, iq, ik, iw, eq, eg):
    h = _mx(_rmsnorm(x_ref[...], g_ref[...]))
    ret_half = RET_DK // 2
    oth_half = MOBA_HD // ROT_DIV // 2

    z = _dot(h, wr_ref[...])
    rq[...] = _rope_apply(z[:, 0:256], tr_ref, ret_half)
    rk[...] = _rope_apply(z[:, 256:512], tr_ref, ret_half) * (RET_DK ** -0.5)
    rv[...] = z[:, 512:1024]
    rg[...] = z[:, 1024:1536]

    z = _dot(h, wm_ref[...])
    mq[...] = _rope_apply(z[:, 0:512], to_ref, oth_half)
    mk[...] = _rope_apply(z[:, 512:1024], to_ref, oth_half)
    mv[...] = z[:, 1024:1536]
    mg[...] = z[:, 1536:2048]

    z = _dot(h, wd_ref[...])
    dq[...] = _rope_apply(z[:, 0:512], to_ref, oth_half)
    dk[...] = _rope_apply(z[:, 512:640], to_ref, oth_half)
    dv[...] = z[:, 640:768]
    dg[...] = z[:, 768:1280]

    z = _dot(h, wi_ref[...])
    iq[...] = _rope_apply(z[:, 0:512], to_ref, oth_half)
    tail = z[:, 512:640]
    ik[...] = _rope_apply(tail, to_ref, oth_half)[:, 0:IDX_D]
    iw[...] = tail[:, IDX_D:IDX_D + IDX_H]

    z = _dot(h, we_ref[...])
    eq[...] = z[:, 0:512]
    eg[...] = z[:, 512:1024]


def _project(x2, g, w, tab_ret, tab_oth, tm):
    n = x2.shape[0]
    nt = tab_ret.shape[1] // tm
    widths = dict(rq=256, rk=256, rv=512, rg=512, mq=512, mk=512, mv=512, mg=512,
                  dq=512, dk=128, dv=128, dg=512, iq=512, ik=64, iw=8, eq=512, eg=512)
    names = list(widths)
    tok = lambda wd: pl.BlockSpec((tm, wd), lambda i: (i, 0))
    tab = pl.BlockSpec((3, tm, LANES), lambda i: (0, i % nt, 0))
    outs = pl.pallas_call(
        _proj_kernel,
        grid=(n // tm,),
        in_specs=[tok(D_MODEL), _const_spec((1, D_MODEL)),
                  _const_spec((D_MODEL, W_RET)), _const_spec((D_MODEL, W_MOBA)),
                  _const_spec((D_MODEL, W_DSA)), _const_spec((D_MODEL, W_IDX_PAD)),
                  _const_spec((D_MODEL, W_MEM)), tab, tab],
        out_specs=[tok(widths[k]) for k in names],
        out_shape=[jax.ShapeDtypeStruct((n, widths[k]), F32) for k in names],
        compiler_params=_params(("parallel",)),
        name="project",
    )(x2, g, w["ret"], w["moba"], w["dsa"], w["idx"], w["mem"], tab_ret, tab_oth)
    return dict(zip(names, outs))


def _ret_kernel(q_ref, k_ref, v_ref, g_ref, s0_ref, dec_ref, qd_ref, kd_ref, cd_ref, gr_ref,
                o_ref, s_ref):
    @pl.when(pl.program_id(1) == 0)
    def _():
        s_ref[...] = s0_ref[...]

    q, k, v = q_ref[...], k_ref[...], v_ref[...]
    outs = []
    for h in range(RET_H):
        qh = _mx(q[:, h * RET_DK:(h + 1) * RET_DK])
        kh = k[:, h * RET_DK:(h + 1) * RET_DK]
        vh = _mx(v[:, h * RET_DV:(h + 1) * RET_DV])
        state = s_ref[0, h]
        inner = _dot_nt(qh, _mx(kh)) * dec_ref[h]
        o = _dot(_mx(inner), vh) + _dot(qh, _mx(state)) * qd_ref[h]
        s_ref[0, h] = cd_ref[h] * state + _dot_tn(_mx(kh * kd_ref[h]), vh)
        mu = jnp.mean(o, axis=-1, keepdims=True)
        d = o - mu
        var = jnp.mean(d * d, axis=-1, keepdims=True)
        outs.append(d * lax.rsqrt(var + EPS))
    o_ref[...] = jnp.concatenate(outs, axis=1) * gr_ref[...] * _silu(g_ref[...])


def _retention(pr, s0, g_ret, nb, t, chunk):
    nc = t // chunk
    lg = jnp.log1p(-jnp.exp2(-5.0 - jnp.arange(RET_H, dtype=F32)))
    n = jnp.arange(chunk, dtype=F32)
    diff = n[:, None] - n[None, :]
    dec = jnp.where(diff[None] >= 0, jnp.exp(jnp.maximum(diff, 0.0)[None] * lg[:, None, None]), 0.0)
    qd = jnp.exp((n[None, :] + 1.0) * lg[:, None])[:, :, None]
    kd = jnp.exp((chunk - 1.0 - n)[None, :] * lg[:, None])[:, :, None]
    cd = jnp.exp(chunk * lg)[:, None, None]
    tok = lambda wd: pl.BlockSpec((chunk, wd), lambda b, c: (b * nc + c, 0))
    st = pl.BlockSpec((1, RET_H, RET_DK, RET_DV), lambda b, c: (b, 0, 0, 0))
    return pl.pallas_call(
        _ret_kernel,
        grid=(nb, nc),
        in_specs=[tok(256), tok(256), tok(512), tok(512), st,
                  _const_spec((RET_H, chunk, chunk)), _const_spec((RET_H, chunk, 1)),
                  _const_spec((RET_H, chunk, 1)), _const_spec((RET_H, 1, 1)), _const_spec((1, 512))],
        out_specs=[tok(512), st],
        out_shape=[jax.ShapeDtypeStruct((nb * t, 512), F32),
                   jax.ShapeDtypeStruct((nb, RET_H, RET_DK, RET_DV), F32)],
        compiler_params=_params(("parallel", "arbitrary")),
        name="retention",
    )(pr["rq"], pr["rk"], pr["rv"], pr["rg"], s0, dec, qd, kd, cd, g_ret)


def _topk_rows_mask(gate, valid_f, col_f, k, n_cols):
    g = jnp.where(valid_f > 0, gate, -jnp.inf)
    sel = jnp.zeros_like(gate)
    for _ in range(k):
        mx = jnp.max(g, axis=-1, keepdims=True)
        idx = jnp.min(jnp.where(g == mx, col_f, float(n_cols)), axis=-1, keepdims=True)
        pick = col_f == idx
        sel = jnp.where(pick, 1.0, sel)
        g = jnp.where(pick, -jnp.inf, g)
    return sel * valid_f


def _sort_key(score):
    bits = lax.bitcast_convert_type(score, jnp.int32)
    key = bits ^ ((bits >> 31) & jnp.int32(0x7FFFFFFF))
    return jnp.where(score == 0.0, 0, key)


def _select_topk_bias(key_ref, bias_ref, nch, valid_fn, rows, n_cols, topk, idx_bits):
    r = rows.stop - rows.start
    col = lax.broadcasted_iota(jnp.int32, (r, n_cols), 1)
    kf = float(topk)

    def count(pred_fn):
        def cb(c, acc):
            return acc + jnp.where(pred_fn(c, key_ref[c, rows, :]), 1.0, 0.0)
        acc = lax.fori_loop(0, nch, cb, jnp.zeros((r, n_cols), F32))
        return jnp.sum(acc, axis=-1, keepdims=True)

    def bit_body(i, carry):
        thr, cnt = carry
        cand = thr + lax.shift_left(jnp.int32(1), 31 - i)
        c2 = count(lambda c, kc: kc >= cand)
        ok = c2 >= kf
        return jnp.where(ok, cand, thr), jnp.where(ok, c2, cnt)

    total = (nch * n_cols + jnp.zeros((r, 1), jnp.int32)).astype(F32)
    thr, cnt = lax.fori_loop(0, 32, bit_body, (jnp.full((r, 1), INT_MIN, jnp.int32), total))

    def tie_search():
        need = kf - count(lambda c, kc: kc > thr)

        def idx_body(i, x):
            cand = x + lax.shift_left(jnp.int32(1), idx_bits - 1 - i)
            c2 = count(lambda c, kc: (kc == thr) & (c * n_cols + col < cand))
            return jnp.where(c2 < need, cand, x)

        return lax.fori_loop(0, idx_bits, idx_body, jnp.zeros((r, 1), jnp.int32)) + 1

    jstar = lax.cond(jnp.max(cnt) > kf, tie_search, lambda: jnp.full((r, 1), INT_MAX, jnp.int32))

    def wb(c, _):
        kc = key_ref[c, rows, :]
        take = (kc > thr) | ((kc == thr) & (c * n_cols + col < jstar))
        bias_ref[c, rows, :] = jnp.where(take & valid_fn(c, rows), 0.0, NEG)
        return 0

    lax.fori_loop(0, nch, wb, 0)


def _softmax_update(s, m_ref, acc_ref, pv_fn):
    m_prev = m_ref[...]
    m_new = jnp.maximum(m_prev, jnp.max(s, axis=-1, keepdims=True))
    alpha = jnp.exp2(m_prev - m_new)
    p = jnp.concatenate([jnp.exp2(s[:, i * LANES:(i + 1) * LANES] - m_new)
                         for i in range(s.shape[1] // LANES)], axis=1)
    acc_ref[...] = alpha * acc_ref[...] + pv_fn(_mx(p))
    m_ref[...] = m_new


def _merge_states(m_ref, acc_ref):
    ms = [m_ref[i] for i in range(N_STATES)]
    m = functools.reduce(jnp.maximum, ms)
    return sum(acc_ref[i] * jnp.exp2(ms[i] - m) for i in range(N_STATES))


def _moba_p_kernel(q_ref, k_ref, v_ref, g_ref, o_ref, kbar_ref, m_ref, acc_ref, *, nblk):
    qi = pl.program_id(2)
    tq = MOBA_BLOCK

    @pl.when(qi == 0)
    def _():
        kbar_ref[...] = jnp.mean(k_ref[...].reshape(nblk, MOBA_BLOCK, LANES), axis=1)

    q2 = q_ref[...]
    lane = lax.broadcasted_iota(jnp.int32, (tq, LANES), 1)
    lo = lane < HALF
    qst = jnp.concatenate([jnp.where(lo, q2, 0.0), jnp.where(lo, 0.0, q2)], axis=0)
    q_att = _mx(qst * (MOBA_HD ** -0.5 * LOG2E))
    blk_i = lax.broadcasted_iota(jnp.int32, (2 * tq, nblk), 1)
    valid_f = jnp.where(blk_i < qi, 1.0, 0.0)
    gate = _dot_nt(_mx(qst), _mx(kbar_ref[...]))
    sel = _topk_rows_mask(gate, valid_f, blk_i.astype(F32), MOBA_TOPK, nblk)
    row = lax.broadcasted_iota(jnp.int32, (2 * tq, tq), 0) & (tq - 1)
    col = lax.broadcasted_iota(jnp.int32, (2 * tq, tq), 1)

    def pv_fn(vb):
        v0 = _mx(jnp.where(lo, vb, 1.0))
        v1 = _mx(jnp.where(lo, 1.0, vb))
        return lambda p: jnp.concatenate([_dot(p[:tq], v0), _dot(p[tq:], v1)], axis=0)

    own0 = pl.multiple_of(qi * tq, tq)
    m_ref[...] = jnp.full(m_ref.shape, NEG, F32)
    acc_ref[...] = jnp.zeros(acc_ref.shape, F32)
    s = jnp.where(col <= row, _dot_nt(q_att, _mx(k_ref[pl.ds(own0, tq), :])), NEG)
    _softmax_update(s, m_ref.at[0], acc_ref.at[0], pv_fn(v_ref[pl.ds(own0, tq), :]))

    sel_mx = _mx(sel)
    pick_row = lax.broadcasted_iota(jnp.int32, (nblk, LANES), 0)

    def body(i, _):
        for par in range(N_STATES):
            n = N_STATES * i + par
            k0 = pl.multiple_of(jnp.minimum(n, nblk - 1) * tq, tq)
            on = _dot(sel_mx, _mx(jnp.where(pick_row == n, 1.0, 0.0)))
            on = jnp.concatenate([on] * (tq // LANES), axis=1) > 0.5
            s = jnp.where(on, _dot_nt(q_att, _mx(k_ref[pl.ds(k0, tq), :])), NEG)
            _softmax_update(s, m_ref.at[par], acc_ref.at[par], pv_fn(v_ref[pl.ds(k0, tq), :]))
        return 0

    lax.fori_loop(0, (qi + N_STATES - 1) // N_STATES, body, 0)
    acc = _merge_states(m_ref, acc_ref)
    a0 = acc[0:tq, :]
    a1 = acc[tq:2 * tq, :]
    o = jnp.where(lo, a0 / _swap_halves(a0), a1 / _swap_halves(a1))
    o_ref[...] = o * _silu(g_ref[...])


def _moba_prompt(pr, nb, t):
    nblk = t // MOBA_BLOCK
    tq = MOBA_BLOCK
    nhp = MOBA_H * MOBA_HD // LANES
    qspec = pl.BlockSpec((tq, LANES), lambda b, hp, qi: (b * nblk + qi, hp))
    kspec = pl.BlockSpec((t, LANES), lambda b, hp, qi: (b, hp))
    return pl.pallas_call(
        functools.partial(_moba_p_kernel, nblk=nblk),
        grid=(nb, nhp, nblk),
        in_specs=[qspec, kspec, kspec, qspec],
        out_specs=qspec,
        out_shape=jax.ShapeDtypeStruct((nb * t, MOBA_H * MOBA_HD), F32),
        scratch_shapes=[pltpu.VMEM((nblk, LANES), F32),
                        pltpu.VMEM((N_STATES, 2 * tq, LANES), F32),
                        pltpu.VMEM((N_STATES, 2 * tq, LANES), F32)],
        compiler_params=_params(("parallel", "parallel", "arbitrary")),
        name="moba_prompt",
    )(pr["mq"], pr["mk"], pr["mv"], pr["mg"])


def _select_topk_bias_t(key_ref, bias_ref, nch, valid_fn, n, topk, idx_bits):
    rowk = lax.broadcasted_iota(jnp.int32, (n, n), 0)
    kf = float(topk)
    sub = 8

    def count(pred_fn):
        def cb(i, acc):
            for par in range(2):
                c = 2 * i + par
                hit = jnp.where(pred_fn(c, key_ref[c]), 1.0, 0.0)
                acc = acc + jnp.sum(hit.reshape(n // sub, sub, n), axis=0)
            return acc
        acc = lax.fori_loop(0, (nch + 1) // 2, cb, jnp.zeros((sub, n), F32))
        return jnp.sum(acc, axis=0, keepdims=True)

    def bit_body(i, carry):
        thr, cnt = carry
        cand = thr + lax.shift_left(jnp.int32(1), 31 - i)
        c2 = count(lambda c, kc: kc >= cand)
        ok = c2 >= kf
        return jnp.where(ok, cand, thr), jnp.where(ok, c2, cnt)

    total = (nch * n + jnp.zeros((1, n), jnp.int32)).astype(F32)
    thr, cnt = lax.fori_loop(0, 32, bit_body, (jnp.full((1, n), INT_MIN, jnp.int32), total))

    def tie_search():
        need = kf - count(lambda c, kc: kc > thr)

        def idx_body(i, x):
            cand = x + lax.shift_left(jnp.int32(1), idx_bits - 1 - i)
            c2 = count(lambda c, kc: (kc == thr) & (c * n + rowk < cand))
            return jnp.where(c2 < need, cand, x)

        return lax.fori_loop(0, idx_bits, idx_body, jnp.zeros((1, n), jnp.int32)) + 1

    jstar = lax.cond(jnp.max(cnt) > kf, tie_search, lambda: jnp.full((1, n), INT_MAX, jnp.int32))

    def wb(c, _):
        kc = key_ref[c]
        take = (kc > thr) | ((kc == thr) & (c * n + rowk < jstar))
        bias_ref[c] = jnp.where(take & valid_fn(c), 0.0, NEG).T
        return 0

    lax.fori_loop(0, nch, wb, 0)


def _dsa_p_kernel(q_ref, iq_ref, iwt_ref, k_ref, v_ref, ki_ref, g_ref, o_ref,
                  key_ref, bias_ref, qs_ref, m_ref, acc_ref, *, topk, idx_bits):
    qi = pl.program_id(1)
    tq = q_ref.shape[0]
    nch = qi + 1
    grp = DSA_H // DSA_KVH
    lane = lax.broadcasted_iota(jnp.int32, (tq, LANES), 1)
    lo = lane < HALF

    def valid_t(c):
        rowk = lax.broadcasted_iota(jnp.int32, (tq, tq), 0)
        colq = lax.broadcasted_iota(jnp.int32, (tq, tq), 1)
        return (rowk <= colq) | (c < qi)

    iq = iq_ref[...]
    wrow = (iwt_ref[...] * (IDX_H ** -0.5)) * (IDX_D ** -0.5)
    iqh = [_mx(iq[:, h * IDX_D:(h + 1) * IDX_D]) for h in range(IDX_H)]

    def score_body(i, _):
        for par in range(2):
            c = 2 * i + par
            cc = jnp.minimum(c, qi)
            kic = _mx(ki_ref[pl.ds(pl.multiple_of(cc * tq, tq), tq), :])
            sc = jnp.zeros((tq, tq), F32)
            for h in range(IDX_H):
                sc = sc + jnp.maximum(_dot_nt(kic, iqh[h]), 0.0) * wrow[h:h + 1, :]
            key = _sort_key(jnp.where(valid_t(c), sc, -jnp.inf))
            key_ref[c] = jnp.where(c < nch, key, INT_MIN)
        return 0

    lax.fori_loop(0, (nch + 1) // 2, score_body, 0)
    _select_topk_bias_t(key_ref, bias_ref, nch, valid_t, tq, topk, idx_bits)

    q = q_ref[...]
    cq = DSA_HD ** -0.5 * LOG2E
    for g in range(DSA_H):
        kvh, j = g // grp, g % grp
        slab = q[:, (g // 2) * LANES:(g // 2 + 1) * LANES]
        if g % 2 != kvh:
            slab = _swap_halves(slab)
        in_kv = lo if kvh == 0 else jnp.logical_not(lo)
        qs_ref[kvh, j * tq:(j + 1) * tq, :] = _mx(jnp.where(in_kv, slab, 0.0) * cq)
    m_ref[...] = jnp.full(m_ref.shape, NEG, F32)
    acc_ref[...] = jnp.zeros(acc_ref.shape, F32)

    def att_body(i, _):
        for par in range(N_STATES):
            c = N_STATES * i + par
            cc = jnp.minimum(c, qi)
            k0 = pl.multiple_of(cc * tq, tq)
            kb = _mx(k_ref[pl.ds(k0, tq), :])
            vb = v_ref[pl.ds(k0, tq), :]
            bias = bias_ref[cc] + jnp.where(c < nch, 0.0, NEG)
            bias4 = jnp.concatenate([bias] * grp, axis=0)
            for kvh in range(DSA_KVH):
                va = _mx(jnp.where(lo, vb, 1.0) if kvh == 0 else jnp.where(lo, 1.0, vb))
                s = _dot_nt(qs_ref[kvh], kb) + bias4
                _softmax_update(s, m_ref.at[par, kvh], acc_ref.at[par, kvh],
                                lambda p, va=va: _dot(p, va))
        return 0

    lax.fori_loop(0, (nch + N_STATES - 1) // N_STATES, att_body, 0)

    g_all = g_ref[...]
    acc = [_merge_states(m_ref.at[:, kvh], acc_ref.at[:, kvh]) for kvh in range(DSA_KVH)]
    for pair in range(DSA_H // 2):
        res = []
        for jj in range(2):
            g = 2 * pair + jj
            kvh, j = g // grp, g % grp
            a = acc[kvh][j * tq:(j + 1) * tq, :]
            o = a / _swap_halves(a)
            res.append(o if jj == kvh else _swap_halves(o))
        sl = slice(pair * LANES, (pair + 1) * LANES)
        o_ref[:, sl] = jnp.where(lo, res[0], res[1]) * _silu(g_all[:, sl])


def _dsa_prompt(pr, nb, t):
    tq = 256
    nq = t // tq
    topk = min(DSA_TOPK, t // 4)
    idx_bits = max(1, math.ceil(math.log2(t)))
    grp = DSA_H // DSA_KVH
    tok = lambda wd: pl.BlockSpec((tq, wd), lambda b, qi: (b * nq + qi, 0))
    seq = lambda wd: pl.BlockSpec((t, wd), lambda b, qi: (b, 0))
    iwt = pr["iw"].T
    return pl.pallas_call(
        functools.partial(_dsa_p_kernel, topk=topk, idx_bits=idx_bits),
        grid=(nb, nq),
        in_specs=[tok(512), tok(512), pl.BlockSpec((IDX_H, tq), lambda b, qi: (0, b * nq + qi)),
                  seq(LANES), seq(LANES), seq(IDX_D), tok(512)],
        out_specs=tok(512),
        out_shape=jax.ShapeDtypeStruct((nb * t, DSA_H * DSA_HD), F32),
        scratch_shapes=[pltpu.VMEM((nq, tq, tq), jnp.int32), pltpu.VMEM((nq, tq, tq), F32),
                        pltpu.VMEM((DSA_KVH, grp * tq, LANES), MXU_DTYPE),
                        pltpu.VMEM((N_STATES, DSA_KVH, grp * tq, LANES), F32),
                        pltpu.VMEM((N_STATES, DSA_KVH, grp * tq, LANES), F32)],
        compiler_params=_params(("parallel", "arbitrary")),
        name="dsa_prompt",
    )(pr["dq"], pr["iq"], iwt, pr["dk"], pr["dv"], pr["ik"], pr["dg"])


def _memkv_kernel(x_ref, g_ref, w_ref, k_ref, v_ref):
    h = _mx(_rmsnorm(x_ref[...], g_ref[...]))
    z = _dot(h, w_ref[...])
    half = MEM_H * MEM_HD
    k_ref[...] = z[:, :half]
    v_ref[...] = z[:, half:]


def _mem_kv(mem2, g, w):
    n = mem2.shape[0]
    tm = 256
    half = MEM_H * MEM_HD
    tok = lambda wd: pl.BlockSpec((tm, wd), lambda i: (i, 0))
    return pl.pallas_call(
        _memkv_kernel,
        grid=(n // tm,),
        in_specs=[tok(D_MODEL), _const_spec((1, D_MODEL)), _const_spec((D_MODEL, 2 * half))],
        out_specs=[tok(half), tok(half)],
        out_shape=[jax.ShapeDtypeStruct((n, half), F32)] * 2,
        compiler_params=_params(("parallel",)),
        name="mem_kv",
    )(mem2, g, w)


def _mem_kernel(q_ref, k_ref, v_ref, g_ref, o_ref):
    q, k, v = q_ref[...], k_ref[...], v_ref[...]
    outs = []
    for h in range(MEM_H):
        sl = slice(h * MEM_HD, (h + 1) * MEM_HD)
        s = _dot_nt(_mx(q[:, sl]), _mx(k[:, sl])) * (MEM_HD ** -0.5)
        m = jnp.max(s, axis=-1, keepdims=True)
        p = jnp.exp(s - m)
        p = p / jnp.sum(p, axis=-1, keepdims=True)
        outs.append(_dot(_mx(p), _mx(v[:, sl])))
    o_ref[...] = jnp.concatenate(outs, axis=1) * _silu(g_ref[...])


def _mem_attend(q, mk, mv, g, nb, t, tq):
    nq = t // tq
    ntok = mk.shape[0] // nb
    tok = pl.BlockSpec((tq, 512), lambda b, i: (b * nq + i, 0))
    kv = pl.BlockSpec((ntok, 512), lambda b, i: (b, 0))
    return pl.pallas_call(
        _mem_kernel,
        grid=(nb, nq),
        in_specs=[tok, kv, kv, tok],
        out_specs=tok,
        out_shape=jax.ShapeDtypeStruct((nb * t, 512), F32),
        compiler_params=_params(("parallel", "arbitrary")),
        name="mem_attend",
    )(q, mk, mv, g)


def _merge_kernel(x_ref, g_ref, wbg_ref, o0, o1, o2, o3, wb_ref, wo_ref, gf_ref, y_ref, *, final):
    x = x_ref[...]
    h = _mx(_rmsnorm(x, g_ref[...]))
    merged = jnp.zeros(x.shape, F32)
    for b, o in enumerate((o0, o1, o2, o3)):
        gate = _sigmoid(_dot(h, wbg_ref[:, b * D_MODEL:(b + 1) * D_MODEL]))
        merged = merged + gate * _dot(_mx(o[...]), wb_ref[b])
    y = x + _dot(_mx(merged), wo_ref[...])
    y_ref[...] = _rmsnorm(y, gf_ref[...]) if final else y


def _merge(x2, g, wbg, outs, wb, wo, gf, final, tm):
    n = x2.shape[0]
    tok = lambda wd: pl.BlockSpec((tm, wd), lambda i: (i, 0))
    return pl.pallas_call(
        functools.partial(_merge_kernel, final=final),
        grid=(n // tm,),
        in_specs=[tok(D_MODEL), _const_spec((1, D_MODEL)), _const_spec((D_MODEL, W_BG)),
                  tok(BR_W), tok(BR_W), tok(BR_W), tok(BR_W),
                  _const_spec((N_BRANCH, BR_W, D_MODEL)), _const_spec((D_MODEL, D_MODEL)),
                  _const_spec((1, D_MODEL))],
        out_specs=tok(D_MODEL),
        out_shape=jax.ShapeDtypeStruct((n, D_MODEL), F32),
        compiler_params=_params(("parallel",)),
        name="merge",
    )(x2, g, wbg, *outs, wb, wo, gf)


def _pages_t(cache):
    nd = cache.ndim
    perm = (0, 1) + tuple(range(3, nd)) + (2,)
    t = jnp.transpose(cache, perm)
    return t.reshape(t.shape[0], t.shape[1], -1, PAGE)


def _page_specs(feat, layer, pps, nj, first_phase):
    def spec(i):
        def imap(b, ph, j, pt):
            jj = jnp.where(ph == 0, j, nj - 1) if first_phase else jnp.where(ph == 0, 0, j)
            return (layer, pt[b, jj * pps + i], 0, 0)
        return pl.BlockSpec((None, None, feat, PAGE), imap)
    return [spec(i) for i in range(pps)]


def _cat_pages(refs):
    return jnp.concatenate([_mx(r[...]) for r in refs], axis=1)


def _pad_rows(x, rows):
    return jnp.concatenate([x, jnp.zeros((rows - x.shape[0], x.shape[1]), x.dtype)], axis=0)


def _moba_s_kernel(pt_ref, qbd_ref, kn_ref, vn_ref, g_ref, *rest, pps, nj):
    kp = rest[:pps]
    vp = rest[pps:2 * pps]
    o_ref = rest[2 * pps]
    s_ref, rl_ref, pown_ref, acc_ref = rest[2 * pps + 1:]
    ph, j = pl.program_id(1), pl.program_id(2)
    nq = kn_ref.shape[0]
    nr = MOBA_H * nq
    wdt = pps * PAGE
    bps = wdt // MOBA_BLOCK
    nblk = nj * bps
    c2 = MOBA_HD ** -0.5 * LOG2E

    @pl.when(ph == 0)
    def _():
        s_ref[j] = _dot(_mx(qbd_ref[0]), _cat_pages(kp))

    @pl.when((ph == 1) & (j == 0))
    def _():
        blane = lax.broadcasted_iota(jnp.int32, (nr, nblk), 1)
        gate = jnp.zeros((nr, nblk), F32)
        bmax = jnp.full((nr, nblk), NEG, F32)
        for jj in range(nj):
            x = s_ref[jj]
            for b in range(bps):
                xb = x[:, b * MOBA_BLOCK:(b + 1) * MOBA_BLOCK]
                here = blane == jj * bps + b
                gate = jnp.where(here, jnp.sum(xb, axis=-1, keepdims=True) * (1.0 / MOBA_BLOCK), gate)
                bmax = jnp.where(here, jnp.max(xb, axis=-1, keepdims=True), bmax)
        sel = _topk_rows_mask(gate, jnp.ones((nr, nblk), F32), blane.astype(F32),
                              min(MOBA_TOPK, nblk), nblk)

        lane = lax.broadcasted_iota(jnp.int32, (nr, LANES), 1)
        qrow = lax.broadcasted_iota(jnp.int32, (nr, LANES), 0) % nq
        s_own = jnp.where(lane <= qrow, _dot_nt(_mx(qbd_ref[0]), _mx(_pad_rows(kn_ref[...], LANES))), NEG)
        m = jnp.maximum(jnp.max(jnp.where(sel > 0, bmax, NEG), axis=-1, keepdims=True),
                        jnp.max(s_own, axis=-1, keepdims=True))
        p_own = jnp.exp2((s_own - m) * c2)
        l = jnp.sum(p_own, axis=-1, keepdims=True)
        for jj in range(nj):
            x = s_ref[jj]
            parts = []
            for b in range(bps):
                on = jnp.sum(jnp.where(blane == jj * bps + b, sel, 0.0), axis=-1, keepdims=True) > 0.0
                xb = x[:, b * MOBA_BLOCK:(b + 1) * MOBA_BLOCK]
                parts.append(jnp.where(on, jnp.exp2((xb - m) * c2), 0.0))
            p = jnp.concatenate(parts, axis=1)
            l = l + jnp.sum(p, axis=-1, keepdims=True)
            s_ref[jj] = p
        rl = 1.0 / l
        rl_ref[...] = jnp.broadcast_to(rl, rl_ref.shape)
        acc_ref[...] = _dot(_mx(p_own * rl), _mx(_pad_rows(vn_ref[...], LANES)))

    @pl.when(ph == 1)
    def _():
        p = _mx(s_ref[j] * rl_ref[:, 0:1])
        acc_ref[...] += _dot_nt(p, _cat_pages(vp))

    @pl.when((ph == 1) & (j == nj - 1))
    def _():
        acc = acc_ref[...]
        parts = [acc[h * nq:(h + 1) * nq, h * MOBA_HD:(h + 1) * MOBA_HD] for h in range(MOBA_H)]
        o_ref[...] = jnp.concatenate(parts, axis=1) * _silu(g_ref[...])
    del pown_ref


def _moba_sample(ps, ck_t, cv_t, layer, page_table, nb, nq):
    n_pages = page_table.shape[1]
    pps = min(SAMPLE_PAGES_PER_STEP, n_pages)
    nj = n_pages // pps
    wd = MOBA_H * MOBA_HD
    nr = MOBA_H * nq
    q4 = ps["mq"].reshape(nb, nq, MOBA_H, MOBA_HD).transpose(0, 2, 1, 3)
    eye = jnp.eye(MOBA_H, dtype=F32)
    qbd = (q4[:, :, :, None, :] * eye[None, :, None, :, None]).reshape(nb, nr, wd)
    tok = pl.BlockSpec((nq, wd), lambda b, ph, j, pt: (b, 0))
    grid_spec = pltpu.PrefetchScalarGridSpec(
        num_scalar_prefetch=1,
        grid=(nb, 2, nj),
        in_specs=[pl.BlockSpec((1, nr, wd), lambda b, ph, j, pt: (b, 0, 0)), tok, tok, tok]
                 + _page_specs(wd, layer, pps, nj, True) + _page_specs(wd, layer, pps, nj, False),
        out_specs=tok,
        scratch_shapes=[pltpu.VMEM((nj, nr, pps * PAGE), F32),
                        pltpu.VMEM((nr, LANES), F32),
                        pltpu.VMEM((8, LANES), F32),
                        pltpu.VMEM((nr, wd), F32)],
    )
    return pl.pallas_call(
        functools.partial(_moba_s_kernel, pps=pps, nj=nj),
        grid_spec=grid_spec,
        out_shape=jax.ShapeDtypeStruct((nb * nq, wd), F32),
        compiler_params=_params(("parallel", "arbitrary", "arbitrary")),
        name="moba_sample",
    )(page_table, qbd, ps["mk"], ps["mv"], ps["mg"], *([ck_t] * pps), *([cv_t] * pps))


def _dsa_s_kernel(pt_ref, qi_ref, wi_ref, qz_ref, ikn_ref, kn_ref, vn_ref, g_ref, *rest,
                  pps, nj, topk, idx_bits):
    ip = rest[:pps]
    kp = rest[pps:2 * pps]
    vp = rest[2 * pps:3 * pps]
    o_ref = rest[3 * pps]
    key_ref, bias_ref, m_ref, l_ref, acc_ref = rest[3 * pps + 1:]
    ph, j = pl.program_id(1), pl.program_id(2)
    nq = ikn_ref.shape[0]
    wdt = pps * PAGE
    scale = DSA_HD ** -0.5
    lane = lax.broadcasted_iota(jnp.int32, (nq, wdt), 1)
    qrow = lax.broadcasted_iota(jnp.int32, (nq, wdt), 0)
    new_valid = lane <= qrow

    def idx_scores(rel):
        rel = jnp.maximum(rel * (IDX_D ** -0.5), 0.0) * wi_ref[0]
        return jnp.sum(rel.reshape(IDX_H, nq, rel.shape[1]), axis=0)

    @pl.when(ph == 0)
    def _():
        key_ref[j] = _sort_key(idx_scores(_dot(_mx(qi_ref[0]), _cat_pages(ip))))

    def attend(s, pv_fn):
        m = m_ref[...]
        m_new = jnp.maximum(m, jnp.max(s, axis=-1, keepdims=True))
        alpha = jnp.exp(m - m_new)
        p = jnp.exp(s - m_new)
        l_ref[...] = alpha * l_ref[...] + jnp.sum(p, axis=-1, keepdims=True)
        acc_ref[...] = alpha * acc_ref[...] + pv_fn(_mx(p))
        m_ref[...] = m_new

    @pl.when((ph == 1) & (j == 0))
    def _():
        ik_new = _mx(_pad_rows(ikn_ref[...], wdt))
        s_new = idx_scores(_dot_nt(_mx(qi_ref[0]), ik_new))
        key_ref[nj] = _sort_key(jnp.where(new_valid, s_new, -jnp.inf))

        def valid(c, rows):
            return new_valid | (c < nj)
        _select_topk_bias(key_ref, bias_ref, nj + 1, valid, slice(0, nq), wdt, topk, idx_bits)

        m_ref[...] = jnp.full(m_ref.shape, NEG, F32)
        l_ref[...] = jnp.zeros(l_ref.shape, F32)
        acc_ref[...] = jnp.zeros(acc_ref.shape, F32)
        k_new = _mx(_pad_rows(kn_ref[...], wdt))
        v_new = _mx(_pad_rows(vn_ref[...], wdt))
        s = _dot_nt(_mx(qz_ref[0]), k_new) * scale + jnp.concatenate([bias_ref[nj]] * DSA_H, axis=0)
        attend(s, lambda p: _dot(p, v_new))

    @pl.when(ph == 1)
    def _():
        vt = _cat_pages(vp)
        s = _dot(_mx(qz_ref[0]), _cat_pages(kp)) * scale + jnp.concatenate([bias_ref[j]] * DSA_H, axis=0)
        attend(s, lambda p: _dot_nt(p, vt))

    @pl.when((ph == 1) & (j == nj - 1))
    def _():
        o = acc_ref[...] / l_ref[...]
        grp = DSA_H // DSA_KVH
        parts = []
        for g in range(DSA_H):
            kvh = g // grp
            parts.append(o[g * nq:(g + 1) * nq, kvh * DSA_HD:(kvh + 1) * DSA_HD])
        o_ref[...] = jnp.concatenate(parts, axis=1) * _silu(g_ref[...])


def _dsa_sample(ps, ck_t, cv_t, ci_t, layer, page_table, nb, nq):
    n_pages = page_table.shape[1]
    pps = min(SAMPLE_PAGES_PER_STEP, n_pages)
    nj = n_pages // pps
    n_keys = n_pages * PAGE + nq
    topk = min(DSA_TOPK, n_keys // 4)
    idx_bits = max(1, math.ceil(math.log2((nj + 1) * pps * PAGE)))
    grp = DSA_H // DSA_KVH
    qi = ps["iq"].reshape(nb, nq, IDX_H, IDX_D).transpose(0, 2, 1, 3).reshape(nb, IDX_H * nq, IDX_D)
    wi = (ps["iw"].reshape(nb, nq, IDX_H).transpose(0, 2, 1) * (IDX_H ** -0.5)).reshape(nb, IDX_H * nq, 1)
    q4 = ps["dq"].reshape(nb, nq, DSA_H, DSA_HD).transpose(0, 2, 1, 3)
    kv_of = (jnp.arange(DSA_H) // grp)[:, None] == jnp.arange(DSA_KVH)[None, :]
    qz = (q4[:, :, :, None, :] * kv_of.astype(F32)[None, :, None, :, None]).reshape(
        nb, DSA_H * nq, DSA_KVH * DSA_HD)
    per_b = lambda r, c: pl.BlockSpec((1, r, c), lambda b, ph, j, pt: (b, 0, 0))
    tok = lambda wd: pl.BlockSpec((nq, wd), lambda b, ph, j, pt: (b, 0))
    grid_spec = pltpu.PrefetchScalarGridSpec(
        num_scalar_prefetch=1,
        grid=(nb, 2, nj),
        in_specs=[per_b(IDX_H * nq, IDX_D), per_b(IDX_H * nq, 1), per_b(DSA_H * nq, LANES),
                  tok(IDX_D), tok(LANES), tok(LANES), tok(512)]
                 + _page_specs(IDX_D, layer, pps, nj, True)
                 + _page_specs(LANES, layer, pps, nj, False)
                 + _page_specs(LANES, layer, pps, nj, False),
        out_specs=tok(512),
        scratch_shapes=[pltpu.VMEM((nj + 1, nq, pps * PAGE), jnp.int32),
                        pltpu.VMEM((nj + 1, nq, pps * PAGE), F32),
                        pltpu.VMEM((DSA_H * nq, 1), F32),
                        pltpu.VMEM((DSA_H * nq, 1), F32),
                        pltpu.VMEM((DSA_H * nq, LANES), F32)],
    )
    return pl.pallas_call(
        functools.partial(_dsa_s_kernel, pps=pps, nj=nj, topk=topk, idx_bits=idx_bits),
        grid_spec=grid_spec,
        out_shape=jax.ShapeDtypeStruct((nb * nq, 512), F32),
        compiler_params=_params(("parallel", "arbitrary", "arbitrary")),
        name="dsa_sample",
    )(page_table, qi, wi, qz, ps["ik"], ps["dk"], ps["dv"], ps["dg"],
      *([ci_t] * pps), *([ck_t] * pps), *([cv_t] * pps))


def _layer_weights(w_in_l):
    cuts = np.cumsum([0, W_RET, W_MOBA, W_DSA, W_IDX, W_MEM, W_BG])
    seg = [w_in_l[:, int(a):int(b)] for a, b in zip(cuts[:-1], cuts[1:])]
    idx = jnp.pad(seg[3], ((0, 0), (0, W_IDX_PAD - W_IDX)))
    return dict(ret=_mx(seg[0]), moba=_mx(seg[1]), dsa=_mx(seg[2]), idx=_mx(idx),
                mem=_mx(seg[4]), bg=_mx(seg[5]))


def kernel(x_prompt, x_sample, cache_moba_k, cache_moba_v, cache_dsa_k, cache_dsa_v, cache_dsa_idx_k,
           cache_mem_k, cache_mem_v, state_ret, page_table, mem_prompt, g_norm, w_in, g_ret, g_mem,
           w_mem_kv, w_branch, w_out, g_final):
    nb_p, t_p, d = x_prompt.shape
    nb_s, t_s, _ = x_sample.shape
    depth = w_in.shape[0]
    past = page_table.shape[1] * PAGE
    pos_p = jnp.arange(t_p, dtype=jnp.int32)
    pos_s = past + jnp.arange(t_s, dtype=jnp.int32)
    tm_p = 256
    n_s = nb_s * t_s
    tab_p = (_rope_tables(pos_p, RET_DK, RET_THETA, RET_DK),
             _rope_tables(pos_p, MOBA_HD // ROT_DIV, ROPE_THETA, MOBA_HD))
    tab_s = tuple(jnp.tile(_rope_tables(pos_s, r, th, 64), (1, nb_s, 1))
                  for r, th in ((RET_DK, RET_THETA), (MOBA_HD // ROT_DIV, ROPE_THETA)))
    s0_p = jnp.zeros((nb_p, RET_H, RET_DK, RET_DV), F32)
    mem2 = mem_prompt.reshape(nb_p * mem_prompt.shape[1], d)
    gf = g_final.reshape(1, d)
    moba_kt, moba_vt = _pages_t(cache_moba_k), _pages_t(cache_moba_v)
    dsa_kt, dsa_vt, dsa_it = _pages_t(cache_dsa_k), _pages_t(cache_dsa_v), _pages_t(cache_dsa_idx_k)

    xp = x_prompt.reshape(nb_p * t_p, d)
    xs = x_sample.reshape(n_s, d)
    acc = {k: [] for k in ("pmk", "pmv", "pdk", "pdv", "pdi", "pret", "pek", "pev",
                           "smk", "smv", "sdk", "sdv", "sdi", "sret")}
    for l in range(depth):
        w = _layer_weights(w_in[l])
        gn = g_norm[l].reshape(1, d)
        gr = g_ret[l].reshape(1, RET_H * RET_DV)
        wb = _mx(w_branch[l])
        wo = _mx(w_out[l])
        final = l == depth - 1

        pr = _project(xp, gn, w, tab_p[0], tab_p[1], tm_p)
        o_ret, st_p = _retention(pr, s0_p, gr, nb_p, t_p, RET_CHUNK)
        o_moba = _moba_prompt(pr, nb_p, t_p)
        o_dsa = _dsa_prompt(pr, nb_p, t_p)
        mk, mv = _mem_kv(mem2, g_mem[l].reshape(1, d), _mx(w_mem_kv[l]))
        o_mem = _mem_attend(pr["eq"], mk, mv, pr["eg"], nb_p, t_p, 256)
        xp = _merge(xp, gn, w["bg"], (o_ret, o_moba, o_dsa, o_mem), wb, wo, gf, final, tm_p)
        acc["pmk"].append(pr["mk"]); acc["pmv"].append(pr["mv"])
        acc["pdk"].append(pr["dk"]); acc["pdv"].append(pr["dv"]); acc["pdi"].append(pr["ik"])
        acc["pret"].append(st_p); acc["pek"].append(mk); acc["pev"].append(mv)

        ps = _project(xs, gn, w, tab_s[0], tab_s[1], n_s)
        o_ret_s, st_s = _retention(ps, state_ret[l], gr, nb_s, t_s, t_s)
        o_moba_s = _moba_sample(ps, moba_kt, moba_vt, l, page_table, nb_s, t_s)
        o_dsa_s = _dsa_sample(ps, dsa_kt, dsa_vt, dsa_it, l, page_table, nb_s, t_s)
        mk_s = cache_mem_k[l].reshape(nb_s * cache_mem_k.shape[2], MEM_H * MEM_HD)
        mv_s = cache_mem_v[l].reshape(nb_s * cache_mem_v.shape[2], MEM_H * MEM_HD)
        o_mem_s = _mem_attend(ps["eq"], mk_s, mv_s, ps["eg"], nb_s, t_s, t_s)
        xs = _merge(xs, gn, w["bg"], (o_ret_s, o_moba_s, o_dsa_s, o_mem_s), wb, wo, gf, final, n_s)
        acc["smk"].append(ps["mk"]); acc["smv"].append(ps["mv"])
        acc["sdk"].append(ps["dk"]); acc["sdv"].append(ps["dv"]); acc["sdi"].append(ps["ik"])
        acc["sret"].append(st_s)

    st = lambda k, shape: jnp.stack(acc[k]).reshape((depth,) + shape)
    ntok = mem_prompt.shape[1]
    return (xp.reshape(nb_p, t_p, d), xs.reshape(nb_s, t_s, d),
            st("pmk", (nb_p, t_p, MOBA_H, MOBA_HD)), st("pmv", (nb_p, t_p, MOBA_H, MOBA_HD)),
            st("pdk", (nb_p, t_p, DSA_KVH, DSA_HD)), st("pdv", (nb_p, t_p, DSA_KVH, DSA_HD)),
            st("pdi", (nb_p, t_p, IDX_D)),
            st("pret", (nb_p, RET_H, RET_DK, RET_DV)),
            st("pek", (nb_p, ntok, MEM_H, MEM_HD)), st("pev", (nb_p, ntok, MEM_H, MEM_HD)),
            st("smk", (nb_s, t_s, MOBA_H, MOBA_HD)), st("smv", (nb_s, t_s, MOBA_H, MOBA_HD)),
            st("sdk", (nb_s, t_s, DSA_KVH, DSA_HD)), st("sdv", (nb_s, t_s, DSA_KVH, DSA_HD)),
            st("sdi", (nb_s, t_s, IDX_D)),
            st("sret", (nb_s, RET_H, RET_DK, RET_DV)))
```

```python
import functools
import math

import jax
import jax.numpy as jnp
import numpy as np
from jax import lax
from jax.experimental import pallas as pl
from jax.experimental.pallas import tpu as pltpu

F32 = jnp.float32
MXU_DTYPE = jnp.bfloat16

D_MODEL = 1024
PAGE = 128
RET_H, RET_DK, RET_DV = 4, 64, 128
RET_CHUNK = 128
RET_THETA = 10000.0
MOBA_H, MOBA_HD, MOBA_BLOCK, MOBA_TOPK = 8, 64, 256, 3
DSA_H, DSA_KVH, DSA_HD = 8, 2, 64
IDX_H, IDX_D = 8, 64
DSA_TOPK = 256
MEM_H, MEM_HD = 4, 128
BR_W = 512
N_BRANCH = 4
ROPE_THETA = 500000.0
ROT_DIV = 4
EPS = 1e-6
LANES = 128
HALF = LANES // 2
NEG = -1e30
INT_MIN = -2147483648
INT_MAX = 2147483647
LOG2E = math.log2(math.e)
VMEM_LIMIT = 56 * 1024 * 1024
SAMPLE_PAGES_PER_STEP = 16
N_STATES = 2

W_RET = 2 * RET_H * RET_DK + RET_H * RET_DV + BR_W
W_MOBA = 3 * MOBA_H * MOBA_HD + BR_W
W_DSA = DSA_H * DSA_HD + 2 * DSA_KVH * DSA_HD + BR_W
W_IDX = IDX_H * IDX_D + IDX_D + IDX_H
W_IDX_PAD = 640
W_MEM = MEM_H * MEM_HD + BR_W
W_BG = N_BRANCH * D_MODEL


def _dot(a, b):
    return jnp.dot(a, b, preferred_element_type=F32)


def _dot_nt(a, b):
    return lax.dot_general(a, b, (((1,), (1,)), ((), ())), preferred_element_type=F32)


def _dot_tn(a, b):
    return lax.dot_general(a, b, (((0,), (0,)), ((), ())), preferred_element_type=F32)


def _mx(a):
    return a.astype(MXU_DTYPE)


def _silu(x):
    return x * (1.0 / (1.0 + jnp.exp(-x)))


def _sigmoid(x):
    return 1.0 / (1.0 + jnp.exp(-x))


def _rmsnorm(x, g):
    return x * lax.rsqrt(jnp.mean(x * x, axis=-1, keepdims=True) + EPS) * g


def _params(sem):
    return pltpu.CompilerParams(dimension_semantics=sem, vmem_limit_bytes=VMEM_LIMIT)


def _const_spec(shape):
    nd = len(shape)
    return pl.BlockSpec(shape, lambda *_: (0,) * nd)


def _swap_halves(x):
    return pltpu.roll(x, HALF, 1)


def _rope_tables(pos, rot_dim, theta, hd):
    half = rot_dim // 2
    inv = jnp.power(theta, -jnp.arange(half, dtype=F32) * 2.0 / rot_dim)
    ang = pos.astype(F32)[:, None] * inv[None, :]
    cos, sin = jnp.cos(ang), jnp.sin(ang)
    t = pos.shape[0]
    z_half = jnp.zeros((t, half), F32)
    rest0 = jnp.zeros((t, hd - rot_dim), F32)
    rest1 = jnp.ones((t, hd - rot_dim), F32)
    c = jnp.concatenate([cos, cos, rest1], axis=1)
    sa = jnp.concatenate([-sin, z_half, rest0], axis=1)
    sb = jnp.concatenate([z_half, sin, rest0], axis=1)
    rep = LANES // hd
    return jnp.stack([jnp.tile(c, (1, rep)), jnp.tile(sa, (1, rep)), jnp.tile(sb, (1, rep))])


def _rope_apply(z, tab_ref, half):
    c, sa, sb = tab_ref[0], tab_ref[1], tab_ref[2]
    outs = []
    for s in range(z.shape[1] // LANES):
        zs = z[:, s * LANES:(s + 1) * LANES]
        outs.append(zs * c + pltpu.roll(zs, LANES - half, 1) * sa + pltpu.roll(zs, half, 1) * sb)
    return outs[0] if len(outs) == 1 else jnp.concatenate(outs, axis=1)


def _proj_kernel(x_ref, g_ref, wr_ref, wm_ref, wd_ref, wi_ref, we_ref, tr_ref, to_ref,
                 rq, rk, rv, rg, mq, mk, mv, mg, dq, dk, dv, dg, iq, ik, iw, eq, eg,
                 mk16, mv16, kbar):
    h = _mx(_rmsnorm(x_ref[...], g_ref[...]))
    ret_half = RET_DK // 2
    oth_half = MOBA_HD // ROT_DIV // 2

    z = _dot(h, wr_ref[...])
    rq[...] = _rope_apply(z[:, 0:256], tr_ref, ret_half)
    rk[...] = _rope_apply(z[:, 256:512], tr_ref, ret_half) * (RET_DK ** -0.5)
    rv[...] = z[:, 512:1024]
    rg[...] = z[:, 1024:1536]

    z = _dot(h, wm_ref[...])
    mq[...] = _rope_apply(z[:, 0:512], to_ref, oth_half)
    k_roped = _rope_apply(z[:, 512:1024], to_ref, oth_half)
    mk[...] = k_roped
    mv[...] = z[:, 1024:1536]
    mg[...] = z[:, 1536:2048]
    mk16[...] = _mx(k_roped)
    mv16[...] = _mx(z[:, 1024:1536])
    kbar[0] = jnp.mean(k_roped, axis=0, keepdims=True)

    z = _dot(h, wd_ref[...])
    dq[...] = _rope_apply(z[:, 0:512], to_ref, oth_half)
    dk[...] = _rope_apply(z[:, 512:640], to_ref, oth_half)
    dv[...] = z[:, 640:768]
    dg[...] = z[:, 768:1280]

    z = _dot(h, wi_ref[...])
    iq[...] = _rope_apply(z[:, 0:512], to_ref, oth_half)
    tail = z[:, 512:640]
    ik[...] = _rope_apply(tail, to_ref, oth_half)[:, 0:IDX_D]
    iw[...] = tail[:, IDX_D:IDX_D + IDX_H]

    z = _dot(h, we_ref[...])
    eq[...] = z[:, 0:512]
    eg[...] = z[:, 512:1024]


def _project(x2, g, w, tab_ret, tab_oth, tm):
    n = x2.shape[0]
    nt = tab_ret.shape[1] // tm
    widths = dict(rq=256, rk=256, rv=512, rg=512, mq=512, mk=512, mv=512, mg=512,
                  dq=512, dk=128, dv=128, dg=512, iq=512, ik=64, iw=8, eq=512, eg=512)
    names = list(widths)
    tok = lambda wd: pl.BlockSpec((tm, wd), lambda i: (i, 0))
    tab = pl.BlockSpec((3, tm, LANES), lambda i: (0, i % nt, 0))
    outs = pl.pallas_call(
        _proj_kernel,
        grid=(n // tm,),
        in_specs=[tok(D_MODEL), _const_spec((1, D_MODEL)),
                  _const_spec((D_MODEL, W_RET)), _const_spec((D_MODEL, W_MOBA)),
                  _const_spec((D_MODEL, W_DSA)), _const_spec((D_MODEL, W_IDX_PAD)),
                  _const_spec((D_MODEL, W_MEM)), tab, tab],
        out_specs=[tok(widths[k]) for k in names]
                  + [tok(512), tok(512), pl.BlockSpec((1, 1, 512), lambda i: (i, 0, 0))],
        out_shape=[jax.ShapeDtypeStruct((n, widths[k]), F32) for k in names]
                  + [jax.ShapeDtypeStruct((n, 512), MXU_DTYPE)] * 2
                  + [jax.ShapeDtypeStruct((n // tm, 1, 512), F32)],
        compiler_params=_params(("parallel",)),
        name="project",
    )(x2, g, w["ret"], w["moba"], w["dsa"], w["idx"], w["mem"], tab_ret, tab_oth)
    return dict(zip(names + ["mk16", "mv16", "kbar"], outs))


def _ret_kernel(q_ref, k_ref, v_ref, g_ref, s0_ref, dec_ref, qd_ref, kd_ref, cd_ref, gr_ref,
                o_ref, s_ref):
    @pl.when(pl.program_id(1) == 0)
    def _():
        s_ref[...] = s0_ref[...]

    q, k, v = q_ref[...], k_ref[...], v_ref[...]
    outs = []
    for h in range(RET_H):
        qh = _mx(q[:, h * RET_DK:(h + 1) * RET_DK])
        kh = k[:, h * RET_DK:(h + 1) * RET_DK]
        vh = _mx(v[:, h * RET_DV:(h + 1) * RET_DV])
        state = s_ref[0, h]
        inner = _dot_nt(qh, _mx(kh)) * dec_ref[h]
        o = _dot(_mx(inner), vh) + _dot(qh, _mx(state)) * qd_ref[h]
        s_ref[0, h] = cd_ref[h] * state + _dot_tn(_mx(kh * kd_ref[h]), vh)
        mu = jnp.mean(o, axis=-1, keepdims=True)
        d = o - mu
        var = jnp.mean(d * d, axis=-1, keepdims=True)
        outs.append(d * lax.rsqrt(var + EPS))
    o_ref[...] = jnp.concatenate(outs, axis=1) * gr_ref[...] * _silu(g_ref[...])


def _retention(pr, s0, g_ret, nb, t, chunk):
    nc = t // chunk
    lg = jnp.log1p(-jnp.exp2(-5.0 - jnp.arange(RET_H, dtype=F32)))
    n = jnp.arange(chunk, dtype=F32)
    diff = n[:, None] - n[None, :]
    dec = jnp.where(diff[None] >= 0, jnp.exp(jnp.maximum(diff, 0.0)[None] * lg[:, None, None]), 0.0)
    qd = jnp.exp((n[None, :] + 1.0) * lg[:, None])[:, :, None]
    kd = jnp.exp((chunk - 1.0 - n)[None, :] * lg[:, None])[:, :, None]
    cd = jnp.exp(chunk * lg)[:, None, None]
    tok = lambda wd: pl.BlockSpec((chunk, wd), lambda b, c: (b * nc + c, 0))
    st = pl.BlockSpec((1, RET_H, RET_DK, RET_DV), lambda b, c: (b, 0, 0, 0))
    return pl.pallas_call(
        _ret_kernel,
        grid=(nb, nc),
        in_specs=[tok(256), tok(256), tok(512), tok(512), st,
                  _const_spec((RET_H, chunk, chunk)), _const_spec((RET_H, chunk, 1)),
                  _const_spec((RET_H, chunk, 1)), _const_spec((RET_H, 1, 1)), _const_spec((1, 512))],
        out_specs=[tok(512), st],
        out_shape=[jax.ShapeDtypeStruct((nb * t, 512), F32),
                   jax.ShapeDtypeStruct((nb, RET_H, RET_DK, RET_DV), F32)],
        compiler_params=_params(("parallel", "arbitrary")),
        name="retention",
    )(pr["rq"], pr["rk"], pr["rv"], pr["rg"], s0, dec, qd, kd, cd, g_ret)


def _topk_rows_mask(gate, valid_f, col_f, k, n_cols):
    g = jnp.where(valid_f > 0, gate, -jnp.inf)
    sel = jnp.zeros_like(gate)
    for _ in range(k):
        mx = jnp.max(g, axis=-1, keepdims=True)
        idx = jnp.min(jnp.where(g == mx, col_f, float(n_cols)), axis=-1, keepdims=True)
        pick = col_f == idx
        sel = jnp.where(pick, 1.0, sel)
        g = jnp.where(pick, -jnp.inf, g)
    return sel * valid_f


def _topk_cols_mask(gate, valid_f, row_f, k, n_rows):
    g = jnp.where(valid_f > 0, gate, -jnp.inf)
    sel = jnp.zeros_like(gate)
    for _ in range(k):
        mx = jnp.max(g, axis=0, keepdims=True)
        idx = jnp.min(jnp.where(g == mx, row_f, float(n_rows)), axis=0, keepdims=True)
        pick = row_f == idx
        sel = jnp.where(pick, 1.0, sel)
        g = jnp.where(pick, -jnp.inf, g)
    return sel * valid_f


def _sort_key(score):
    bits = lax.bitcast_convert_type(score, jnp.int32)
    key = bits ^ ((bits >> 31) & jnp.int32(0x7FFFFFFF))
    return jnp.where(score == 0.0, 0, key)


def _select_topk_bias(key_ref, bias_ref, nch, valid_fn, rows, n_cols, topk, idx_bits):
    r = rows.stop - rows.start
    col = lax.broadcasted_iota(jnp.int32, (r, n_cols), 1)
    kf = float(topk)

    def count(pred_fn):
        def cb(c, acc):
            return acc + jnp.where(pred_fn(c, key_ref[c, rows, :]), 1.0, 0.0)
        acc = lax.fori_loop(0, nch, cb, jnp.zeros((r, n_cols), F32))
        return jnp.sum(acc, axis=-1, keepdims=True)

    def bit_body(i, carry):
        thr, cnt = carry
        cand = thr + lax.shift_left(jnp.int32(1), 31 - i)
        c2 = count(lambda c, kc: kc >= cand)
        ok = c2 >= kf
        return jnp.where(ok, cand, thr), jnp.where(ok, c2, cnt)

    total = (nch * n_cols + jnp.zeros((r, 1), jnp.int32)).astype(F32)
    thr, cnt = lax.fori_loop(0, 32, bit_body, (jnp.full((r, 1), INT_MIN, jnp.int32), total))

    def tie_search():
        need = kf - count(lambda c, kc: kc > thr)

        def idx_body(i, x):
            cand = x + lax.shift_left(jnp.int32(1), idx_bits - 1 - i)
            c2 = count(lambda c, kc: (kc == thr) & (c * n_cols + col < cand))
            return jnp.where(c2 < need, cand, x)

        return lax.fori_loop(0, idx_bits, idx_body, jnp.zeros((r, 1), jnp.int32)) + 1

    jstar = lax.cond(jnp.max(cnt) > kf, tie_search, lambda: jnp.full((r, 1), INT_MAX, jnp.int32))

    def wb(c, _):
        kc = key_ref[c, rows, :]
        take = (kc > thr) | ((kc == thr) & (c * n_cols + col < jstar))
        bias_ref[c, rows, :] = jnp.where(take & valid_fn(c, rows), 0.0, NEG)
        return 0

    lax.fori_loop(0, nch, wb, 0)


def _softmax_update(s, m_ref, acc_ref, pv_fn):
    m_prev = m_ref[...]
    m_new = jnp.maximum(m_prev, jnp.max(s, axis=-1, keepdims=True))
    alpha = jnp.exp2(m_prev - m_new)
    p = jnp.concatenate([jnp.exp2(s[:, i * LANES:(i + 1) * LANES] - m_new)
                         for i in range(s.shape[1] // LANES)], axis=1)
    acc_ref[...] = alpha * acc_ref[...] + pv_fn(_mx(p))
    m_ref[...] = m_new


def _merge_states(m_ref, acc_ref):
    ms = [m_ref[i] for i in range(N_STATES)]
    m = functools.reduce(jnp.maximum, ms)
    return sum(acc_ref[i] * jnp.exp2(ms[i] - m) for i in range(N_STATES))


def _moba_p_kernel(q_ref, k_ref, v_ref, kbar_ref, g_ref, o_ref, qa_ref, m_ref, acc_ref, *, nblk):
    qi = pl.program_id(1)
    tq = MOBA_BLOCK
    nhp = MOBA_H * MOBA_HD // LANES
    lane = lax.broadcasted_iota(jnp.int32, (tq, LANES), 1)
    lo = lane < HALF
    blk_r = lax.broadcasted_iota(jnp.int32, (nblk, 2 * tq), 0)
    valid_f = jnp.where(blk_r < qi, 1.0, 0.0)
    row = lax.broadcasted_iota(jnp.int32, (2 * tq, tq), 0) & (tq - 1)
    col = lax.broadcasted_iota(jnp.int32, (2 * tq, tq), 1)
    one = jnp.ones((), MXU_DTYPE)
    zero = jnp.zeros((), MXU_DTYPE)
    cq = MOBA_HD ** -0.5 * LOG2E

    def pv_fn(vb):
        v0 = jnp.where(lo, vb, one)
        v1 = jnp.where(lo, one, vb)
        return lambda p: jnp.concatenate([_dot(p[:tq], v0), _dot(p[tq:], v1)], axis=0)

    def scores(qa, kb, n):
        k_h0 = jnp.where(lo, kb, _mx(jnp.where(lane == HALF + n, 1.0, 0.0)))
        k_h1 = jnp.where(lo, _mx(jnp.where(lane == n, 1.0, 0.0)), kb)
        return jnp.concatenate([_dot_nt(qa[:tq], k_h0), _dot_nt(qa[tq:], k_h1)], axis=0)

    own0 = pl.multiple_of(qi * tq, tq)
    m_ref[...] = jnp.full(m_ref.shape, NEG, F32)
    acc_ref[...] = jnp.zeros(acc_ref.shape, F32)
    kbar = kbar_ref[...].reshape(nblk, MOBA_H * MOBA_HD)
    pad = jnp.zeros((tq, HALF - nblk), F32)
    for hp in range(nhp):
        sl = slice(hp * LANES, (hp + 1) * LANES)
        q2 = q_ref[:, sl]
        qst = jnp.concatenate([jnp.where(lo, q2, 0.0), jnp.where(lo, 0.0, q2)], axis=0)
        gate_t = _dot_nt(_mx(kbar[:, sl]), _mx(qst))
        sel_t = _topk_cols_mask(gate_t, valid_f, blk_r.astype(F32), MOBA_TOPK, nblk)
        bias = jnp.where(sel_t > 0.0, 0.0, NEG).T
        qa = jnp.concatenate(
            [jnp.concatenate([q2[:, :HALF] * cq, bias[:tq], pad], axis=1),
             jnp.concatenate([bias[tq:], pad, q2[:, HALF:] * cq], axis=1)], axis=0)
        qa_ref[hp] = _mx(qa)
        s = jnp.where(col <= row, scores(qa_ref[hp], k_ref[pl.ds(own0, tq), sl], -LANES), NEG)
        _softmax_update(s, m_ref.at[hp, 0], acc_ref.at[hp, 0], pv_fn(v_ref[pl.ds(own0, tq), sl]))

    def body(i, _):
        for par in range(N_STATES):
            n = N_STATES * i + par
            k0 = pl.multiple_of(jnp.minimum(n, nblk - 1) * tq, tq)
            for hp in range(nhp):
                sl = slice(hp * LANES, (hp + 1) * LANES)
                s = scores(qa_ref[hp], k_ref[pl.ds(k0, tq), sl], n)
                _softmax_update(s, m_ref.at[hp, par], acc_ref.at[hp, par], pv_fn(v_ref[pl.ds(k0, tq), sl]))
        return 0

    lax.fori_loop(0, (qi + N_STATES - 1) // N_STATES, body, 0)
    for hp in range(nhp):
        sl = slice(hp * LANES, (hp + 1) * LANES)
        acc = _merge_states(m_ref.at[hp], acc_ref.at[hp])
        a0 = acc[0:tq, :]
        a1 = acc[tq:2 * tq, :]
        o = jnp.where(lo, a0 / _swap_halves(a0), a1 / _swap_halves(a1))
        o_ref[:, sl] = o * _silu(g_ref[:, sl])


def _moba_prompt(pr, nb, t):
    nblk = t // MOBA_BLOCK
    tq = MOBA_BLOCK
    wd = MOBA_H * MOBA_HD
    nhp = wd // LANES
    qspec = pl.BlockSpec((tq, wd), lambda b, qi: (b * nblk + qi, 0))
    kspec = pl.BlockSpec((t, wd), lambda b, qi: (b, 0))
    return pl.pallas_call(
        functools.partial(_moba_p_kernel, nblk=nblk),
        grid=(nb, nblk),
        in_specs=[qspec, kspec, kspec, pl.BlockSpec((nblk, 1, wd), lambda b, qi: (b, 0, 0)), qspec],
        out_specs=qspec,
        out_shape=jax.ShapeDtypeStruct((nb * t, wd), F32),
        scratch_shapes=[pltpu.VMEM((nhp, 2 * tq, LANES), MXU_DTYPE),
                        pltpu.VMEM((nhp, N_STATES, 2 * tq, LANES), F32),
                        pltpu.VMEM((nhp, N_STATES, 2 * tq, LANES), F32)],
        compiler_params=_params(("parallel", "arbitrary")),
        name="moba_prompt",
    )(pr["mq"], pr["mk16"], pr["mv16"], pr["kbar"], pr["mg"])


def _select_topk_bias_t(key_ref, bias_ref, nch, valid_fn, n, topk, idx_bits):
    rowk = lax.broadcasted_iota(jnp.int32, (n, n), 0)
    kf = float(topk)
    sub = 8

    def count(pred_fn):
        def cb(i, acc):
            for par in range(2):
                c = 2 * i + par
                hit = jnp.where(pred_fn(c, key_ref[c]), 1.0, 0.0)
                acc = acc + jnp.sum(hit.reshape(n // sub, sub, n), axis=0)
            return acc
        acc = lax.fori_loop(0, (nch + 1) // 2, cb, jnp.zeros((sub, n), F32))
        return jnp.sum(acc, axis=0, keepdims=True)

    def bit_body(i, carry):
        thr, cnt = carry
        cand = thr + lax.shift_left(jnp.int32(1), 31 - i)
        c2 = count(lambda c, kc: kc >= cand)
        ok = c2 >= kf
        return jnp.where(ok, cand, thr), jnp.where(ok, c2, cnt)

    total = (nch * n + jnp.zeros((1, n), jnp.int32)).astype(F32)
    thr, cnt = lax.fori_loop(0, 32, bit_body, (jnp.full((1, n), INT_MIN, jnp.int32), total))

    def tie_search():
        need = kf - count(lambda c, kc: kc > thr)

        def idx_body(i, x):
            cand = x + lax.shift_left(jnp.int32(1), idx_bits - 1 - i)
            c2 = count(lambda c, kc: (kc == thr) & (c * n + rowk < cand))
            return jnp.where(c2 < need, cand, x)

        return lax.fori_loop(0, idx_bits, idx_body, jnp.zeros((1, n), jnp.int32)) + 1

    jstar = lax.cond(jnp.max(cnt) > kf, tie_search, lambda: jnp.full((1, n), INT_MAX, jnp.int32))

    def wb(c, _):
        kc = key_ref[c]
        take = (kc > thr) | ((kc == thr) & (c * n + rowk < jstar))
        bias_ref[c] = jnp.where(take & valid_fn(c), 0.0, NEG).T
        return 0

    lax.fori_loop(0, nch, wb, 0)


def _dsa_p_kernel(q_ref, iq_ref, iwt_ref, k_ref, v_ref, ki_ref, g_ref, o_ref,
                  key_ref, bias_ref, qs_ref, m_ref, acc_ref, *, topk, idx_bits):
    qi = pl.program_id(1)
    tq = q_ref.shape[0]
    nch = qi + 1
    grp = DSA_H // DSA_KVH
    lane = lax.broadcasted_iota(jnp.int32, (tq, LANES), 1)
    lo = lane < HALF

    def valid_t(c):
        rowk = lax.broadcasted_iota(jnp.int32, (tq, tq), 0)
        colq = lax.broadcasted_iota(jnp.int32, (tq, tq), 1)
        return (rowk <= colq) | (c < qi)

    iq = iq_ref[...]
    wrow = (iwt_ref[...] * (IDX_H ** -0.5)) * (IDX_D ** -0.5)
    iqh = [_mx(iq[:, h * IDX_D:(h + 1) * IDX_D]) for h in range(IDX_H)]

    def score_body(i, _):
        for par in range(2):
            c = 2 * i + par
            cc = jnp.minimum(c, qi)
            kic = _mx(ki_ref[pl.ds(pl.multiple_of(cc * tq, tq), tq), :])
            sc = jnp.zeros((tq, tq), F32)
            for h in range(IDX_H):
                sc = sc + jnp.maximum(_dot_nt(kic, iqh[h]), 0.0) * wrow[h:h + 1, :]
            key = _sort_key(jnp.where(valid_t(c), sc, -jnp.inf))
            key_ref[c] = jnp.where(c < nch, key, INT_MIN)
        return 0

    lax.fori_loop(0, (nch + 1) // 2, score_body, 0)
    _select_topk_bias_t(key_ref, bias_ref, nch, valid_t, tq, topk, idx_bits)

    q = q_ref[...]
    cq = DSA_HD ** -0.5 * LOG2E
    for g in range(DSA_H):
        kvh, j = g // grp, g % grp
        slab = q[:, (g // 2) * LANES:(g // 2 + 1) * LANES]
        if g % 2 != kvh:
            slab = _swap_halves(slab)
        in_kv = lo if kvh == 0 else jnp.logical_not(lo)
        qs_ref[kvh, j * tq:(j + 1) * tq, :] = _mx(jnp.where(in_kv, slab, 0.0) * cq)
    m_ref[...] = jnp.full(m_ref.shape, NEG, F32)
    acc_ref[...] = jnp.zeros(acc_ref.shape, F32)

    def att_body(i, _):
        for par in range(N_STATES):
            c = N_STATES * i + par
            cc = jnp.minimum(c, qi)
            k0 = pl.multiple_of(cc * tq, tq)
            kb = _mx(k_ref[pl.ds(k0, tq), :])
            vb = v_ref[pl.ds(k0, tq), :]
            bias = bias_ref[cc] + jnp.where(c < nch, 0.0, NEG)
            bias4 = jnp.concatenate([bias] * grp, axis=0)
            for kvh in range(DSA_KVH):
                va = _mx(jnp.where(lo, vb, 1.0) if kvh == 0 else jnp.where(lo, 1.0, vb))
                s = _dot_nt(qs_ref[kvh], kb) + bias4
                _softmax_update(s, m_ref.at[par, kvh], acc_ref.at[par, kvh],
                                lambda p, va=va: _dot(p, va))
        return 0

    lax.fori_loop(0, (nch + N_STATES - 1) // N_STATES, att_body, 0)

    g_all = g_ref[...]
    acc = [_merge_states(m_ref.at[:, kvh], acc_ref.at[:, kvh]) for kvh in range(DSA_KVH)]
    for pair in range(DSA_H // 2):
        res = []
        for jj in range(2):
            g = 2 * pair + jj
            kvh, j = g // grp, g % grp
            a = acc[kvh][j * tq:(j + 1) * tq, :]
            o = a / _swap_halves(a)
            res.append(o if jj == kvh else _swap_halves(o))
        sl = slice(pair * LANES, (pair + 1) * LANES)
        o_ref[:, sl] = jnp.where(lo, res[0], res[1]) * _silu(g_all[:, sl])


def _dsa_prompt(pr, nb, t):
    tq = 256
    nq = t // tq
    topk = min(DSA_TOPK, t // 4)
    idx_bits = max(1, math.ceil(math.log2(t)))
    grp = DSA_H // DSA_KVH
    tok = lambda wd: pl.BlockSpec((tq, wd), lambda b, qi: (b * nq + qi, 0))
    seq = lambda wd: pl.BlockSpec((t, wd), lambda b, qi: (b, 0))
    iwt = pr["iw"].T
    return pl.pallas_call(
        functools.partial(_dsa_p_kernel, topk=topk, idx_bits=idx_bits),
        grid=(nb, nq),
        in_specs=[tok(512), tok(512), pl.BlockSpec((IDX_H, tq), lambda b, qi: (0, b * nq + qi)),
                  seq(LANES), seq(LANES), seq(IDX_D), tok(512)],
        out_specs=tok(512),
        out_shape=jax.ShapeDtypeStruct((nb * t, DSA_H * DSA_HD), F32),
        scratch_shapes=[pltpu.VMEM((nq, tq, tq), jnp.int32), pltpu.VMEM((nq, tq, tq), F32),
                        pltpu.VMEM((DSA_KVH, grp * tq, LANES), MXU_DTYPE),
                        pltpu.VMEM((N_STATES, DSA_KVH, grp * tq, LANES), F32),
                        pltpu.VMEM((N_STATES, DSA_KVH, grp * tq, LANES), F32)],
        compiler_params=_params(("parallel", "arbitrary")),
        name="dsa_prompt",
    )(pr["dq"], pr["iq"], iwt, pr["dk"], pr["dv"], pr["ik"], pr["dg"])


def _memkv_kernel(x_ref, g_ref, w_ref, k_ref, v_ref):
    h = _mx(_rmsnorm(x_ref[...], g_ref[...]))
    z = _dot(h, w_ref[...])
    half = MEM_H * MEM_HD
    k_ref[...] = z[:, :half]
    v_ref[...] = z[:, half:]


def _mem_kv(mem2, g, w):
    n = mem2.shape[0]
    tm = 256
    half = MEM_H * MEM_HD
    tok = lambda wd: pl.BlockSpec((tm, wd), lambda i: (i, 0))
    return pl.pallas_call(
        _memkv_kernel,
        grid=(n // tm,),
        in_specs=[tok(D_MODEL), _const_spec((1, D_MODEL)), _const_spec((D_MODEL, 2 * half))],
        out_specs=[tok(half), tok(half)],
        out_shape=[jax.ShapeDtypeStruct((n, half), F32)] * 2,
        compiler_params=_params(("parallel",)),
        name="mem_kv",
    )(mem2, g, w)


def _mem_kernel(q_ref, k_ref, v_ref, g_ref, o_ref):
    q, k, v = q_ref[...], k_ref[...], v_ref[...]
    outs = []
    for h in range(MEM_H):
        sl = slice(h * MEM_HD, (h + 1) * MEM_HD)
        s = _dot_nt(_mx(q[:, sl]), _mx(k[:, sl])) * (MEM_HD ** -0.5)
        m = jnp.max(s, axis=-1, keepdims=True)
        p = jnp.exp(s - m)
        p = p / jnp.sum(p, axis=-1, keepdims=True)
        outs.append(_dot(_mx(p), _mx(v[:, sl])))
    o_ref[...] = jnp.concatenate(outs, axis=1) * _silu(g_ref[...])


def _mem_attend(q, mk, mv, g, nb, t, tq):
    nq = t // tq
    ntok = mk.shape[0] // nb
    tok = pl.BlockSpec((tq, 512), lambda b, i: (b * nq + i, 0))
    kv = pl.BlockSpec((ntok, 512), lambda b, i: (b, 0))
    return pl.pallas_call(
        _mem_kernel,
        grid=(nb, nq),
        in_specs=[tok, kv, kv, tok],
        out_specs=tok,
        out_shape=jax.ShapeDtypeStruct((nb * t, 512), F32),
        compiler_params=_params(("parallel", "arbitrary")),
        name="mem_attend",
    )(q, mk, mv, g)


def _merge_kernel(x_ref, g_ref, wbg_ref, o0, o1, o2, o3, wb_ref, wo_ref, gf_ref, y_ref, *, final):
    x = x_ref[...]
    h = _mx(_rmsnorm(x, g_ref[...]))
    merged = jnp.zeros(x.shape, F32)
    for b, o in enumerate((o0, o1, o2, o3)):
        gate = _sigmoid(_dot(h, wbg_ref[:, b * D_MODEL:(b + 1) * D_MODEL]))
        merged = merged + gate * _dot(_mx(o[...]), wb_ref[b])
    y = x + _dot(_mx(merged), wo_ref[...])
    y_ref[...] = _rmsnorm(y, gf_ref[...]) if final else y


def _merge(x2, g, wbg, outs, wb, wo, gf, final, tm):
    n = x2.shape[0]
    tok = lambda wd: pl.BlockSpec((tm, wd), lambda i: (i, 0))
    return pl.pallas_call(
        functools.partial(_merge_kernel, final=final),
        grid=(n // tm,),
        in_specs=[tok(D_MODEL), _const_spec((1, D_MODEL)), _const_spec((D_MODEL, W_BG)),
                  tok(BR_W), tok(BR_W), tok(BR_W), tok(BR_W),
                  _const_spec((N_BRANCH, BR_W, D_MODEL)), _const_spec((D_MODEL, D_MODEL)),
                  _const_spec((1, D_MODEL))],
        out_specs=tok(D_MODEL),
        out_shape=jax.ShapeDtypeStruct((n, D_MODEL), F32),
        compiler_params=_params(("parallel",)),
        name="merge",
    )(x2, g, wbg, *outs, wb, wo, gf)


def _pages_t(cache):
    nd = cache.ndim
    perm = (0, 1) + tuple(range(3, nd)) + (2,)
    t = jnp.transpose(cache, perm)
    return t.reshape(t.shape[0], t.shape[1], -1, PAGE)


def _page_copy(cache_ref, buf_ref, sem_ref, slot, layer, page, i):
    return pltpu.make_async_copy(cache_ref.at[layer, page],
                                 buf_ref.at[slot, :, i * PAGE:(i + 1) * PAGE], sem_ref.at[slot])


def _start_pages(cache_ref, buf_ref, sem_ref, slot, layer, pt_ref, b, j, pps):
    for i in range(pps):
        _page_copy(cache_ref, buf_ref, sem_ref, slot, layer, pt_ref[b, j * pps + i], i).start()


def _wait_pages(cache_ref, buf_ref, sem_ref, slot, layer, pps):
    for i in range(pps):
        _page_copy(cache_ref, buf_ref, sem_ref, slot, layer, 0, i).wait()


def _sample_step(nj):
    b, ph, j = pl.program_id(0), pl.program_id(1), pl.program_id(2)
    return (b * 2 + ph) * nj + j, pl.num_programs(0) * 2 * nj


def _prefetch_pages(t, total, issue):
    @pl.when(t == 0)
    def _():
        issue(t)

    @pl.when(t + 1 < total)
    def _():
        issue(t + 1)


def _pad_rows(x, rows):
    return jnp.concatenate([x, jnp.zeros((rows - x.shape[0], x.shape[1]), x.dtype)], axis=0)


def _moba_s_kernel(pt_ref, qbd_ref, kn_ref, vn_ref, g_ref, ck_ref, cv_ref, o_ref,
                   s_ref, rl_ref, acc_ref, buf_ref, sem_ref, *, pps, nj, layer):
    ph, j = pl.program_id(1), pl.program_id(2)
    nq = kn_ref.shape[0]
    nr = MOBA_H * nq
    wdt = pps * PAGE
    bps = wdt // MOBA_BLOCK
    nblk = nj * bps
    c2 = MOBA_HD ** -0.5 * LOG2E
    t, total = _sample_step(nj)
    slot = t % 2

    def issue(tt):
        bb, pp, jj = tt // (2 * nj), (tt // nj) % 2, tt % nj
        for phase, cache_ref in enumerate((ck_ref, cv_ref)):
            @pl.when(pp == phase)
            def _(cache_ref=cache_ref):
                _start_pages(cache_ref, buf_ref, sem_ref, tt % 2, layer, pt_ref, bb, jj, pps)

    _prefetch_pages(t, total, issue)
    _wait_pages(ck_ref, buf_ref, sem_ref, slot, layer, pps)

    @pl.when(ph == 0)
    def _():
        s_ref[j] = _dot(_mx(qbd_ref[0]), _mx(buf_ref[slot]))

    @pl.when((ph == 1) & (j == 0))
    def _():
        blane = lax.broadcasted_iota(jnp.int32, (nr, nblk), 1)
        gate = jnp.zeros((nr, nblk), F32)
        bmax = jnp.full((nr, nblk), NEG, F32)
        for jj in range(nj):
            x = s_ref[jj]
            for b in range(bps):
                xb = x[:, b * MOBA_BLOCK:(b + 1) * MOBA_BLOCK]
                here = blane == jj * bps + b
                gate = jnp.where(here, jnp.sum(xb, axis=-1, keepdims=True) * (1.0 / MOBA_BLOCK), gate)
                bmax = jnp.where(here, jnp.max(xb, axis=-1, keepdims=True), bmax)
        sel = _topk_rows_mask(gate, jnp.ones((nr, nblk), F32), blane.astype(F32),
                              min(MOBA_TOPK, nblk), nblk)

        lane = lax.broadcasted_iota(jnp.int32, (nr, LANES), 1)
        qrow = lax.broadcasted_iota(jnp.int32, (nr, LANES), 0) % nq
        s_own = jnp.where(lane <= qrow, _dot_nt(_mx(qbd_ref[0]), _mx(_pad_rows(kn_ref[...], LANES))), NEG)
        m = jnp.maximum(jnp.max(jnp.where(sel > 0, bmax, NEG), axis=-1, keepdims=True),
                        jnp.max(s_own, axis=-1, keepdims=True))
        p_own = jnp.exp2((s_own - m) * c2)
        l = jnp.sum(p_own, axis=-1, keepdims=True)
        for jj in range(nj):
            x = s_ref[jj]
            parts = []
            for b in range(bps):
                on = jnp.sum(jnp.where(blane == jj * bps + b, sel, 0.0), axis=-1, keepdims=True) > 0.0
                xb = x[:, b * MOBA_BLOCK:(b + 1) * MOBA_BLOCK]
                parts.append(jnp.where(on, jnp.exp2((xb - m) * c2), 0.0))
            p = jnp.concatenate(parts, axis=1)
            l = l + jnp.sum(p, axis=-1, keepdims=True)
            s_ref[jj] = p
        rl = 1.0 / l
        rl_ref[...] = jnp.broadcast_to(rl, rl_ref.shape)
        acc_ref[...] = _dot(_mx(p_own * rl), _mx(_pad_rows(vn_ref[...], LANES)))

    @pl.when(ph == 1)
    def _():
        p = _mx(s_ref[j] * rl_ref[:, 0:1])
        acc_ref[...] += _dot_nt(p, _mx(buf_ref[slot]))

    @pl.when((ph == 1) & (j == nj - 1))
    def _():
        acc = acc_ref[...]
        parts = [acc[h * nq:(h + 1) * nq, h * MOBA_HD:(h + 1) * MOBA_HD] for h in range(MOBA_H)]
        o_ref[...] = jnp.concatenate(parts, axis=1) * _silu(g_ref[...])


def _moba_sample(ps, ck_t, cv_t, layer, page_table, nb, nq):
    n_pages = page_table.shape[1]
    pps = min(SAMPLE_PAGES_PER_STEP, n_pages)
    nj = n_pages // pps
    wd = MOBA_H * MOBA_HD
    nr = MOBA_H * nq
    q4 = ps["mq"].reshape(nb, nq, MOBA_H, MOBA_HD).transpose(0, 2, 1, 3)
    eye = jnp.eye(MOBA_H, dtype=F32)
    qbd = (q4[:, :, :, None, :] * eye[None, :, None, :, None]).reshape(nb, nr, wd)
    tok = pl.BlockSpec((nq, wd), lambda b, ph, j, pt: (b, 0))
    hbm = pl.BlockSpec(memory_space=pl.ANY)
    grid_spec = pltpu.PrefetchScalarGridSpec(
        num_scalar_prefetch=1,
        grid=(nb, 2, nj),
        in_specs=[pl.BlockSpec((1, nr, wd), lambda b, ph, j, pt: (b, 0, 0)), tok, tok, tok, hbm, hbm],
        out_specs=tok,
        scratch_shapes=[pltpu.VMEM((nj, nr, pps * PAGE), F32),
                        pltpu.VMEM((nr, LANES), F32),
                        pltpu.VMEM((nr, wd), F32),
                        pltpu.VMEM((2, wd, pps * PAGE), F32),
                        pltpu.SemaphoreType.DMA((2,))],
    )
    return pl.pallas_call(
        functools.partial(_moba_s_kernel, pps=pps, nj=nj, layer=layer),
        grid_spec=grid_spec,
        out_shape=jax.ShapeDtypeStruct((nb * nq, wd), F32),
        compiler_params=_params(("arbitrary", "arbitrary", "arbitrary")),
        name="moba_sample",
    )(page_table, qbd, ps["mk"], ps["mv"], ps["mg"], ck_t, cv_t)


def _dsa_s_kernel(pt_ref, qi_ref, wi_ref, qz_ref, ikn_ref, kn_ref, vn_ref, g_ref,
                  ci_ref, ck_ref, cv_ref, o_ref, key_ref, bias_ref, m_ref, l_ref, acc_ref,
                  ibuf_ref, kbuf_ref, vbuf_ref, isem_ref, ksem_ref, vsem_ref,
                  *, pps, nj, topk, idx_bits, layer):
    ph, j = pl.program_id(1), pl.program_id(2)
    nq = ikn_ref.shape[0]
    wdt = pps * PAGE
    scale = DSA_HD ** -0.5
    lane = lax.broadcasted_iota(jnp.int32, (nq, wdt), 1)
    qrow = lax.broadcasted_iota(jnp.int32, (nq, wdt), 0)
    new_valid = lane <= qrow
    t, total = _sample_step(nj)
    slot = t % 2

    def issue(tt):
        bb, pp, jj = tt // (2 * nj), (tt // nj) % 2, tt % nj

        @pl.when(pp == 0)
        def _():
            _start_pages(ci_ref, ibuf_ref, isem_ref, tt % 2, layer, pt_ref, bb, jj, pps)

        @pl.when(pp == 1)
        def _():
            _start_pages(ck_ref, kbuf_ref, ksem_ref, tt % 2, layer, pt_ref, bb, jj, pps)
            _start_pages(cv_ref, vbuf_ref, vsem_ref, tt % 2, layer, pt_ref, bb, jj, pps)

    _prefetch_pages(t, total, issue)

    def idx_scores(rel):
        rel = jnp.maximum(rel * (IDX_D ** -0.5), 0.0) * wi_ref[0]
        return jnp.sum(rel.reshape(IDX_H, nq, rel.shape[1]), axis=0)

    @pl.when(ph == 0)
    def _():
        _wait_pages(ci_ref, ibuf_ref, isem_ref, slot, layer, pps)
        key_ref[j] = _sort_key(idx_scores(_dot(_mx(qi_ref[0]), _mx(ibuf_ref[slot]))))

    def attend(s, pv_fn):
        m = m_ref[...]
        m_new = jnp.maximum(m, jnp.max(s, axis=-1, keepdims=True))
        alpha = jnp.exp(m - m_new)
        p = jnp.exp(s - m_new)
        l_ref[...] = alpha * l_ref[...] + jnp.sum(p, axis=-1, keepdims=True)
        acc_ref[...] = alpha * acc_ref[...] + pv_fn(_mx(p))
        m_ref[...] = m_new

    @pl.when((ph == 1) & (j == 0))
    def _():
        ik_new = _mx(_pad_rows(ikn_ref[...], wdt))
        s_new = idx_scores(_dot_nt(_mx(qi_ref[0]), ik_new))
        key_ref[nj] = _sort_key(jnp.where(new_valid, s_new, -jnp.inf))

        def valid(c, rows):
            return new_valid | (c < nj)
        _select_topk_bias(key_ref, bias_ref, nj + 1, valid, slice(0, nq), wdt, topk, idx_bits)

        m_ref[...] = jnp.full(m_ref.shape, NEG, F32)
        l_ref[...] = jnp.zeros(l_ref.shape, F32)
        acc_ref[...] = jnp.zeros(acc_ref.shape, F32)
        k_new = _mx(_pad_rows(kn_ref[...], wdt))
        v_new = _mx(_pad_rows(vn_ref[...], wdt))
        s = _dot_nt(_mx(qz_ref[0]), k_new) * scale + jnp.concatenate([bias_ref[nj]] * DSA_H, axis=0)
        attend(s, lambda p: _dot(p, v_new))

    @pl.when(ph == 1)
    def _():
        _wait_pages(ck_ref, kbuf_ref, ksem_ref, slot, layer, pps)
        _wait_pages(cv_ref, vbuf_ref, vsem_ref, slot, layer, pps)
        s = (_dot(_mx(qz_ref[0]), _mx(kbuf_ref[slot])) * scale
             + jnp.concatenate([bias_ref[j]] * DSA_H, axis=0))
        attend(s, lambda p: _dot_nt(p, _mx(vbuf_ref[slot])))

    @pl.when((ph == 1) & (j == nj - 1))
    def _():
        o = acc_ref[...] / l_ref[...]
        grp = DSA_H // DSA_KVH
        parts = []
        for g in range(DSA_H):
            kvh = g // grp
            parts.append(o[g * nq:(g + 1) * nq, kvh * DSA_HD:(kvh + 1) * DSA_HD])
        o_ref[...] = jnp.concatenate(parts, axis=1) * _silu(g_ref[...])


def _dsa_sample(ps, ck_t, cv_t, ci_t, layer, page_table, nb, nq):
    n_pages = page_table.shape[1]
    pps = min(SAMPLE_PAGES_PER_STEP, n_pages)
    nj = n_pages // pps
    n_keys = n_pages * PAGE + nq
    topk = min(DSA_TOPK, n_keys // 4)
    idx_bits = max(1, math.ceil(math.log2((nj + 1) * pps * PAGE)))
    grp = DSA_H // DSA_KVH
    qi = ps["iq"].reshape(nb, nq, IDX_H, IDX_D).transpose(0, 2, 1, 3).reshape(nb, IDX_H * nq, IDX_D)
    wi = (ps["iw"].reshape(nb, nq, IDX_H).transpose(0, 2, 1) * (IDX_H ** -0.5)).reshape(nb, IDX_H * nq, 1)
    q4 = ps["dq"].reshape(nb, nq, DSA_H, DSA_HD).transpose(0, 2, 1, 3)
    kv_of = (jnp.arange(DSA_H) // grp)[:, None] == jnp.arange(DSA_KVH)[None, :]
    qz = (q4[:, :, :, None, :] * kv_of.astype(F32)[None, :, None, :, None]).reshape(
        nb, DSA_H * nq, DSA_KVH * DSA_HD)
    per_b = lambda r, c: pl.BlockSpec((1, r, c), lambda b, ph, j, pt: (b, 0, 0))
    tok = lambda wd: pl.BlockSpec((nq, wd), lambda b, ph, j, pt: (b, 0))
    hbm = pl.BlockSpec(memory_space=pl.ANY)
    grid_spec = pltpu.PrefetchScalarGridSpec(
        num_scalar_prefetch=1,
        grid=(nb, 2, nj),
        in_specs=[per_b(IDX_H * nq, IDX_D), per_b(IDX_H * nq, 1), per_b(DSA_H * nq, LANES),
                  tok(IDX_D), tok(LANES), tok(LANES), tok(512), hbm, hbm, hbm],
        out_specs=tok(512),
        scratch_shapes=[pltpu.VMEM((nj + 1, nq, pps * PAGE), jnp.int32),
                        pltpu.VMEM((nj + 1, nq, pps * PAGE), F32),
                        pltpu.VMEM((DSA_H * nq, 1), F32),
                        pltpu.VMEM((DSA_H * nq, 1), F32),
                        pltpu.VMEM((DSA_H * nq, LANES), F32),
                        pltpu.VMEM((2, IDX_D, pps * PAGE), F32),
                        pltpu.VMEM((2, LANES, pps * PAGE), F32),
                        pltpu.VMEM((2, LANES, pps * PAGE), F32),
                        pltpu.SemaphoreType.DMA((2,)),
                        pltpu.SemaphoreType.DMA((2,)),
                        pltpu.SemaphoreType.DMA((2,))],
    )
    return pl.pallas_call(
        functools.partial(_dsa_s_kernel, pps=pps, nj=nj, topk=topk, idx_bits=idx_bits, layer=layer),
        grid_spec=grid_spec,
        out_shape=jax.ShapeDtypeStruct((nb * nq, 512), F32),
        compiler_params=_params(("arbitrary", "arbitrary", "arbitrary")),
        name="dsa_sample",
    )(page_table, qi, wi, qz, ps["ik"], ps["dk"], ps["dv"], ps["dg"], ci_t, ck_t, cv_t)


def _layer_weights(w_in_l):
    cuts = np.cumsum([0, W_RET, W_MOBA, W_DSA, W_IDX, W_MEM, W_BG])
    seg = [w_in_l[:, int(a):int(b)] for a, b in zip(cuts[:-1], cuts[1:])]
    idx = jnp.pad(seg[3], ((0, 0), (0, W_IDX_PAD - W_IDX)))
    return dict(ret=_mx(seg[0]), moba=_mx(seg[1]), dsa=_mx(seg[2]), idx=_mx(idx),
                mem=_mx(seg[4]), bg=_mx(seg[5]))


def kernel(x_prompt, x_sample, cache_moba_k, cache_moba_v, cache_dsa_k, cache_dsa_v, cache_dsa_idx_k,
           cache_mem_k, cache_mem_v, state_ret, page_table, mem_prompt, g_norm, w_in, g_ret, g_mem,
           w_mem_kv, w_branch, w_out, g_final):
    nb_p, t_p, d = x_prompt.shape
    nb_s, t_s, _ = x_sample.shape
    depth = w_in.shape[0]
    past = page_table.shape[1] * PAGE
    pos_p = jnp.arange(t_p, dtype=jnp.int32)
    pos_s = past + jnp.arange(t_s, dtype=jnp.int32)
    tm_p = 256
    n_s = nb_s * t_s
    tab_p = (_rope_tables(pos_p, RET_DK, RET_THETA, RET_DK),
             _rope_tables(pos_p, MOBA_HD // ROT_DIV, ROPE_THETA, MOBA_HD))
    tab_s = tuple(jnp.tile(_rope_tables(pos_s, r, th, 64), (1, nb_s, 1))
                  for r, th in ((RET_DK, RET_THETA), (MOBA_HD // ROT_DIV, ROPE_THETA)))
    s0_p = jnp.zeros((nb_p, RET_H, RET_DK, RET_DV), F32)
    mem2 = mem_prompt.reshape(nb_p * mem_prompt.shape[1], d)
    gf = g_final.reshape(1, d)
    moba_kt, moba_vt = _pages_t(cache_moba_k), _pages_t(cache_moba_v)
    dsa_kt, dsa_vt, dsa_it = _pages_t(cache_dsa_k), _pages_t(cache_dsa_v), _pages_t(cache_dsa_idx_k)

    xp = x_prompt.reshape(nb_p * t_p, d)
    xs = x_sample.reshape(n_s, d)
    acc = {k: [] for k in ("pmk", "pmv", "pdk", "pdv", "pdi", "pret", "pek", "pev",
                           "smk", "smv", "sdk", "sdv", "sdi", "sret")}
    for l in range(depth):
        w = _layer_weights(w_in[l])
        gn = g_norm[l].reshape(1, d)
        gr = g_ret[l].reshape(1, RET_H * RET_DV)
        wb = _mx(w_branch[l])
        wo = _mx(w_out[l])
        final = l == depth - 1

        pr = _project(xp, gn, w, tab_p[0], tab_p[1], tm_p)
        o_ret, st_p = _retention(pr, s0_p, gr, nb_p, t_p, RET_CHUNK)
        o_moba = _moba_prompt(pr, nb_p, t_p)
        o_dsa = _dsa_prompt(pr, nb_p, t_p)
        mk, mv = _mem_kv(mem2, g_mem[l].reshape(1, d), _mx(w_mem_kv[l]))
        o_mem = _mem_attend(pr["eq"], mk, mv, pr["eg"], nb_p, t_p, 256)
        xp = _merge(xp, gn, w["bg"], (o_ret, o_moba, o_dsa, o_mem), wb, wo, gf, final, tm_p)
        acc["pmk"].append(pr["mk"]); acc["pmv"].append(pr["mv"])
        acc["pdk"].append(pr["dk"]); acc["pdv"].append(pr["dv"]); acc["pdi"].append(pr["ik"])
        acc["pret"].append(st_p); acc["pek"].append(mk); acc["pev"].append(mv)

        ps = _project(xs, gn, w, tab_s[0], tab_s[1], n_s)
        o_ret_s, st_s = _retention(ps, state_ret[l], gr, nb_s, t_s, t_s)
        o_moba_s = _moba_sample(ps, moba_kt, moba_vt, l, page_table, nb_s, t_s)
        o_dsa_s = _dsa_sample(ps, dsa_kt, dsa_vt, dsa_it, l, page_table, nb_s, t_s)
        mk_s = cache_mem_k[l].reshape(nb_s * cache_mem_k.shape[2], MEM_H * MEM_HD)
        mv_s = cache_mem_v[l].reshape(nb_s * cache_mem_v.shape[2], MEM_H * MEM_HD)
        o_mem_s = _mem_attend(ps["eq"], mk_s, mv_s, ps["eg"], nb_s, t_s, t_s)
        xs = _merge(xs, gn, w["bg"], (o_ret_s, o_moba_s, o_dsa_s, o_mem_s), wb, wo, gf, final, n_s)
        acc["smk"].append(ps["mk"]); acc["smv"].append(ps["mv"])
        acc["sdk"].append(ps["dk"]); acc["sdv"].append(ps["dv"]); acc["sdi"].append(ps["ik"])
        acc["sret"].append(st_s)

    st = lambda k, shape: jnp.stack(acc[k]).reshape((depth,) + shape)
    ntok = mem_prompt.shape[1]
    return (xp.reshape(nb_p, t_p, d), xs.reshape(nb_s, t_s, d),
            st("pmk", (nb_p, t_p, MOBA_H, MOBA_HD)), st("pmv", (nb_p, t_p, MOBA_H, MOBA_HD)),
            st("pdk", (nb_p, t_p, DSA_KVH, DSA_HD)), st("pdv", (nb_p, t_p, DSA_KVH, DSA_HD)),
            st("pdi", (nb_p, t_p, IDX_D)),
            st("pret", (nb_p, RET_H, RET_DK, RET_DV)),
            st("pek", (nb_p, ntok, MEM_H, MEM_HD)), st("pev", (nb_p, ntok, MEM_H, MEM_HD)),
            st("smk", (nb_s, t_s, MOBA_H, MOBA_HD)), st("smv", (nb_s, t_s, MOBA_H, MOBA_HD)),
            st("sdk", (nb_s, t_s, DSA_KVH, DSA_HD)), st("sdv", (nb_s, t_s, DSA_KVH, DSA_HD)),
            st("sdi", (nb_s, t_s, IDX_D)),
            st("sret", (nb_s, RET_H, RET_DK, RET_DV)))
```

```python
import functools
import math

import jax
import jax.numpy as jnp
import numpy as np
from jax import lax
from jax.experimental import pallas as pl
from jax.experimental.pallas import tpu as pltpu

F32 = jnp.float32
MXU_DTYPE = jnp.bfloat16

D_MODEL = 1024
PAGE = 128
RET_H, RET_DK, RET_DV = 4, 64, 128
RET_CHUNK = 128
RET_THETA = 10000.0
MOBA_H, MOBA_HD, MOBA_BLOCK, MOBA_TOPK = 8, 64, 256, 3
DSA_H, DSA_KVH, DSA_HD = 8, 2, 64
IDX_H, IDX_D = 8, 64
DSA_TOPK = 256
MEM_H, MEM_HD = 4, 128
BR_W = 512
N_BRANCH = 4
ROPE_THETA = 500000.0
ROT_DIV = 4
EPS = 1e-6
LANES = 128
HALF = LANES // 2
NEG = -1e30
INT_MIN = -2147483648
INT_MAX = 2147483647
LOG2E = math.log2(math.e)
VMEM_LIMIT = 56 * 1024 * 1024
SAMPLE_PAGES_PER_STEP = 32
RET_SEQS_PER_STEP = 8
MOBA_PAIRS_PER_STEP = 4
N_STATES = 2

W_RET = 2 * RET_H * RET_DK + RET_H * RET_DV + BR_W
W_MOBA = 3 * MOBA_H * MOBA_HD + BR_W
W_DSA = DSA_H * DSA_HD + 2 * DSA_KVH * DSA_HD + BR_W
W_IDX = IDX_H * IDX_D + IDX_D + IDX_H
W_IDX_PAD = 640
W_MEM = MEM_H * MEM_HD + BR_W
W_BG = N_BRANCH * D_MODEL


def _dot(a, b):
    return jnp.dot(a, b, preferred_element_type=F32)


def _dot_nt(a, b):
    return lax.dot_general(a, b, (((1,), (1,)), ((), ())), preferred_element_type=F32)


def _dot_tn(a, b):
    return lax.dot_general(a, b, (((0,), (0,)), ((), ())), preferred_element_type=F32)


def _mx(a):
    return a.astype(MXU_DTYPE)


def _silu(x):
    return x * (1.0 / (1.0 + jnp.exp(-x)))


def _sigmoid(x):
    return 1.0 / (1.0 + jnp.exp(-x))


def _rmsnorm(x, g):
    return x * lax.rsqrt(jnp.mean(x * x, axis=-1, keepdims=True) + EPS) * g


def _params(sem):
    return pltpu.CompilerParams(dimension_semantics=sem, vmem_limit_bytes=VMEM_LIMIT)


def _const_spec(shape):
    nd = len(shape)
    return pl.BlockSpec(shape, lambda *_: (0,) * nd)


def _swap_halves(x):
    return pltpu.roll(x, HALF, 1)


def _rope_tables(pos, rot_dim, theta, hd):
    half = rot_dim // 2
    inv = jnp.power(theta, -jnp.arange(half, dtype=F32) * 2.0 / rot_dim)
    ang = pos.astype(F32)[:, None] * inv[None, :]
    cos, sin = jnp.cos(ang), jnp.sin(ang)
    t = pos.shape[0]
    z_half = jnp.zeros((t, half), F32)
    rest0 = jnp.zeros((t, hd - rot_dim), F32)
    rest1 = jnp.ones((t, hd - rot_dim), F32)
    c = jnp.concatenate([cos, cos, rest1], axis=1)
    sa = jnp.concatenate([-sin, z_half, rest0], axis=1)
    sb = jnp.concatenate([z_half, sin, rest0], axis=1)
    rep = LANES // hd
    return jnp.stack([jnp.tile(c, (1, rep)), jnp.tile(sa, (1, rep)), jnp.tile(sb, (1, rep))])


def _rope_apply(z, tab_ref, half):
    c, sa, sb = tab_ref[0], tab_ref[1], tab_ref[2]
    outs = []
    for s in range(z.shape[1] // LANES):
        zs = z[:, s * LANES:(s + 1) * LANES]
        outs.append(zs * c + pltpu.roll(zs, LANES - half, 1) * sa + pltpu.roll(zs, half, 1) * sb)
    return outs[0] if len(outs) == 1 else jnp.concatenate(outs, axis=1)


def _proj_kernel(x_ref, g_ref, wr_ref, wm_ref, wd_ref, wi_ref, we_ref, tr_ref, to_ref,
                 rq, rk, rv, rg, mq, mk, mv, mg, dq, dk, dv, dg, iq, ik, iw, eq, eg,
                 mk16, mv16, kbar):
    h = _mx(_rmsnorm(x_ref[...], g_ref[...]))
    ret_half = RET_DK // 2
    oth_half = MOBA_HD // ROT_DIV // 2

    z = _dot(h, wr_ref[...])
    rq[...] = _rope_apply(z[:, 0:256], tr_ref, ret_half)
    rk[...] = _rope_apply(z[:, 256:512], tr_ref, ret_half) * (RET_DK ** -0.5)
    rv[...] = z[:, 512:1024]
    rg[...] = z[:, 1024:1536]

    z = _dot(h, wm_ref[...])
    mq[...] = _rope_apply(z[:, 0:512], to_ref, oth_half)
    k_roped = _rope_apply(z[:, 512:1024], to_ref, oth_half)
    mk[...] = k_roped
    mv[...] = z[:, 1024:1536]
    mg[...] = z[:, 1536:2048]
    mk16[...] = _mx(k_roped)
    mv16[...] = _mx(z[:, 1024:1536])
    kbar[0] = jnp.mean(k_roped, axis=0, keepdims=True)

    z = _dot(h, wd_ref[...])
    dq[...] = _rope_apply(z[:, 0:512], to_ref, oth_half)
    dk[...] = _rope_apply(z[:, 512:640], to_ref, oth_half)
    dv[...] = z[:, 640:768]
    dg[...] = z[:, 768:1280]

    z = _dot(h, wi_ref[...])
    iq[...] = _rope_apply(z[:, 0:512], to_ref, oth_half)
    tail = z[:, 512:640]
    ik[...] = _rope_apply(tail, to_ref, oth_half)[:, 0:IDX_D]
    iw[...] = tail[:, IDX_D:IDX_D + IDX_H]

    z = _dot(h, we_ref[...])
    eq[...] = z[:, 0:512]
    eg[...] = z[:, 512:1024]


def _project(x2, g, w, tab_ret, tab_oth, tm):
    n = x2.shape[0]
    nt = tab_ret.shape[1] // tm
    widths = dict(rq=256, rk=256, rv=512, rg=512, mq=512, mk=512, mv=512, mg=512,
                  dq=512, dk=128, dv=128, dg=512, iq=512, ik=64, iw=8, eq=512, eg=512)
    names = list(widths)
    tok = lambda wd: pl.BlockSpec((tm, wd), lambda i: (i, 0))
    tab = pl.BlockSpec((3, tm, LANES), lambda i: (0, i % nt, 0))
    outs = pl.pallas_call(
        _proj_kernel,
        grid=(n // tm,),
        in_specs=[tok(D_MODEL), _const_spec((1, D_MODEL)),
                  _const_spec((D_MODEL, W_RET)), _const_spec((D_MODEL, W_MOBA)),
                  _const_spec((D_MODEL, W_DSA)), _const_spec((D_MODEL, W_IDX_PAD)),
                  _const_spec((D_MODEL, W_MEM)), tab, tab],
        out_specs=[tok(widths[k]) for k in names]
                  + [tok(512), tok(512), pl.BlockSpec((1, 1, 512), lambda i: (i, 0, 0))],
        out_shape=[jax.ShapeDtypeStruct((n, widths[k]), F32) for k in names]
                  + [jax.ShapeDtypeStruct((n, 512), MXU_DTYPE)] * 2
                  + [jax.ShapeDtypeStruct((n // tm, 1, 512), F32)],
        compiler_params=_params(("parallel",)),
        name="project",
    )(x2, g, w["ret"], w["moba"], w["dsa"], w["idx"], w["mem"], tab_ret, tab_oth)
    return dict(zip(names + ["mk16", "mv16", "kbar"], outs))


def _ret_kernel(q_ref, k_ref, v_ref, g_ref, s0_ref, dec_ref, qd_ref, kd_ref, cd_ref, gr_ref,
                o_ref, s_ref):
    @pl.when(pl.program_id(1) == 0)
    def _():
        s_ref[...] = s0_ref[...]

    for b in range(q_ref.shape[0]):
        q, k, v = q_ref[b], k_ref[b], v_ref[b]
        outs = []
        for h in range(RET_H):
            qh = _mx(q[:, h * RET_DK:(h + 1) * RET_DK])
            kh = k[:, h * RET_DK:(h + 1) * RET_DK]
            vh = _mx(v[:, h * RET_DV:(h + 1) * RET_DV])
            state = s_ref[b, h]
            inner = _dot_nt(qh, _mx(kh)) * dec_ref[h]
            o = _dot(_mx(inner), vh) + _dot(qh, _mx(state)) * qd_ref[h]
            s_ref[b, h] = cd_ref[h] * state + _dot_tn(_mx(kh * kd_ref[h]), vh)
            mu = jnp.mean(o, axis=-1, keepdims=True)
            d = o - mu
            var = jnp.mean(d * d, axis=-1, keepdims=True)
            outs.append(d * lax.rsqrt(var + EPS))
        o_ref[b] = jnp.concatenate(outs, axis=1) * gr_ref[...] * _silu(g_ref[b])


def _retention(pr, s0, g_ret, nb, t, chunk):
    nc = t // chunk
    lg = jnp.log1p(-jnp.exp2(-5.0 - jnp.arange(RET_H, dtype=F32)))
    n = jnp.arange(chunk, dtype=F32)
    diff = n[:, None] - n[None, :]
    dec = jnp.where(diff[None] >= 0, jnp.exp(jnp.maximum(diff, 0.0)[None] * lg[:, None, None]), 0.0)
    qd = jnp.exp((n[None, :] + 1.0) * lg[:, None])[:, :, None]
    kd = jnp.exp((chunk - 1.0 - n)[None, :] * lg[:, None])[:, :, None]
    cd = jnp.exp(chunk * lg)[:, None, None]
    sps = math.gcd(nb, RET_SEQS_PER_STEP)
    tok = lambda wd: pl.BlockSpec((sps, chunk, wd), lambda b, c: (b, c, 0))
    st = pl.BlockSpec((sps, RET_H, RET_DK, RET_DV), lambda b, c: (b, 0, 0, 0))
    seq3 = lambda a: a.reshape(nb, t, a.shape[-1])
    out, state = pl.pallas_call(
        _ret_kernel,
        grid=(nb // sps, nc),
        in_specs=[tok(256), tok(256), tok(512), tok(512), st,
                  _const_spec((RET_H, chunk, chunk)), _const_spec((RET_H, chunk, 1)),
                  _const_spec((RET_H, chunk, 1)), _const_spec((RET_H, 1, 1)), _const_spec((1, 512))],
        out_specs=[tok(512), st],
        out_shape=[jax.ShapeDtypeStruct((nb, t, 512), F32),
                   jax.ShapeDtypeStruct((nb, RET_H, RET_DK, RET_DV), F32)],
        compiler_params=_params(("parallel", "arbitrary")),
        name="retention",
    )(seq3(pr["rq"]), seq3(pr["rk"]), seq3(pr["rv"]), seq3(pr["rg"]), s0, dec, qd, kd, cd, g_ret)
    return out.reshape(nb * t, 512), state


def _topk_rows_mask(gate, valid_f, col_f, k, n_cols):
    g = jnp.where(valid_f > 0, gate, -jnp.inf)
    sel = jnp.zeros_like(gate)
    for _ in range(k):
        mx = jnp.max(g, axis=-1, keepdims=True)
        idx = jnp.min(jnp.where(g == mx, col_f, float(n_cols)), axis=-1, keepdims=True)
        pick = col_f == idx
        sel = jnp.where(pick, 1.0, sel)
        g = jnp.where(pick, -jnp.inf, g)
    return sel * valid_f


def _topk_cols_mask(gate, valid_f, row_f, k, n_rows):
    g = jnp.where(valid_f > 0, gate, -jnp.inf)
    sel = jnp.zeros_like(gate)
    for _ in range(k):
        mx = jnp.max(g, axis=0, keepdims=True)
        idx = jnp.min(jnp.where(g == mx, row_f, float(n_rows)), axis=0, keepdims=True)
        pick = row_f == idx
        sel = jnp.where(pick, 1.0, sel)
        g = jnp.where(pick, -jnp.inf, g)
    return sel * valid_f


def _sort_key(score):
    bits = lax.bitcast_convert_type(score, jnp.int32)
    key = bits ^ ((bits >> 31) & jnp.int32(0x7FFFFFFF))
    return jnp.where(score == 0.0, 0, key)


def _select_topk_bias(key_ref, bias_ref, nch, valid_fn, rows, n_cols, topk, idx_bits):
    r = rows.stop - rows.start
    col = lax.broadcasted_iota(jnp.int32, (r, n_cols), 1)
    kf = float(topk)

    def count(pred_fn):
        def cb(c, acc):
            return acc + jnp.where(pred_fn(c, key_ref[c, rows, :]), 1.0, 0.0)
        acc = lax.fori_loop(0, nch, cb, jnp.zeros((r, n_cols), F32))
        return jnp.sum(acc, axis=-1, keepdims=True)

    def bit_body(i, carry):
        thr, cnt = carry
        cand = thr + lax.shift_left(jnp.int32(1), 31 - i)
        c2 = count(lambda c, kc: kc >= cand)
        ok = c2 >= kf
        return jnp.where(ok, cand, thr), jnp.where(ok, c2, cnt)

    total = (nch * n_cols + jnp.zeros((r, 1), jnp.int32)).astype(F32)
    thr, cnt = lax.fori_loop(0, 32, bit_body, (jnp.full((r, 1), INT_MIN, jnp.int32), total))

    def tie_search():
        need = kf - count(lambda c, kc: kc > thr)

        def idx_body(i, x):
            cand = x + lax.shift_left(jnp.int32(1), idx_bits - 1 - i)
            c2 = count(lambda c, kc: (kc == thr) & (c * n_cols + col < cand))
            return jnp.where(c2 < need, cand, x)

        return lax.fori_loop(0, idx_bits, idx_body, jnp.zeros((r, 1), jnp.int32)) + 1

    jstar = lax.cond(jnp.max(cnt) > kf, tie_search, lambda: jnp.full((r, 1), INT_MAX, jnp.int32))

    def wb(c, _):
        kc = key_ref[c, rows, :]
        take = (kc > thr) | ((kc == thr) & (c * n_cols + col < jstar))
        bias_ref[c, rows, :] = jnp.where(take & valid_fn(c, rows), 0.0, NEG)
        return 0

    lax.fori_loop(0, nch, wb, 0)


def _softmax_update(s, m_ref, acc_ref, pv_fn):
    m_prev = m_ref[...]
    m_new = jnp.maximum(m_prev, jnp.max(s, axis=-1, keepdims=True))
    alpha = jnp.exp2(m_prev - m_new)
    p = jnp.concatenate([jnp.exp2(s[:, i * LANES:(i + 1) * LANES] - m_new)
                         for i in range(s.shape[1] // LANES)], axis=1)
    acc_ref[...] = alpha * acc_ref[...] + pv_fn(_mx(p))
    m_ref[...] = m_new


def _merge_states(m_ref, acc_ref):
    ms = [m_ref[i] for i in range(N_STATES)]
    m = functools.reduce(jnp.maximum, ms)
    return sum(acc_ref[i] * jnp.exp2(ms[i] - m) for i in range(N_STATES))


def _moba_p_kernel(q_ref, k_ref, v_ref, kbar_ref, g_ref, o_ref, qa_ref, m_ref, acc_ref, *, nblk):
    qi = pl.program_id(2)
    tq = MOBA_BLOCK
    nhp = q_ref.shape[1] // LANES
    lane = lax.broadcasted_iota(jnp.int32, (tq, LANES), 1)
    lo = lane < HALF
    blk_r = lax.broadcasted_iota(jnp.int32, (nblk, 2 * tq), 0)
    valid_f = jnp.where(blk_r < qi, 1.0, 0.0)
    causal = (lax.broadcasted_iota(jnp.int32, (tq, tq), 1)
              <= lax.broadcasted_iota(jnp.int32, (tq, tq), 0))
    one = jnp.ones((), MXU_DTYPE)
    cq = MOBA_HD ** -0.5 * LOG2E

    def block_update(hp, par, k0, n, mask):
        sl = slice(hp * LANES, (hp + 1) * LANES)
        kb = k_ref[pl.ds(k0, tq), sl]
        vb = v_ref[pl.ds(k0, tq), sl]
        s, v = [], []
        for head in range(2):
            rows = slice(head * tq, (head + 1) * tq)
            ind = _mx(jnp.where(lane == (HALF + n if head == 0 else n), 1.0, 0.0))
            k_h = jnp.where(lo, kb, ind) if head == 0 else jnp.where(lo, ind, kb)
            v.append(jnp.where(lo, vb, one) if head == 0 else jnp.where(lo, one, vb))
            s.append(_dot_nt(qa_ref[hp, rows, :], k_h))
            if mask is None:
                _softmax_update(s[head], m_ref.at[hp, par, rows], acc_ref.at[hp, par, rows],
                                lambda p, v_h=v[head]: _dot(p, v_h))
        if mask is not None:
            both = jnp.concatenate([jnp.where(mask, x, NEG) for x in s], axis=0)
            _softmax_update(both, m_ref.at[hp, par], acc_ref.at[hp, par],
                            lambda p: jnp.concatenate([_dot(p[:tq], v[0]), _dot(p[tq:], v[1])], axis=0))

    own0 = pl.multiple_of(qi * tq, tq)
    m_ref[...] = jnp.full(m_ref.shape, NEG, F32)
    acc_ref[...] = jnp.zeros(acc_ref.shape, F32)
    kbar = kbar_ref[...].reshape(nblk, nhp * LANES)
    pad = jnp.zeros((tq, HALF - nblk), F32)
    for hp in range(nhp):
        sl = slice(hp * LANES, (hp + 1) * LANES)
        q2 = q_ref[:, sl]
        qst = jnp.concatenate([jnp.where(lo, q2, 0.0), jnp.where(lo, 0.0, q2)], axis=0)
        gate_t = _dot_nt(_mx(kbar[:, sl]), _mx(qst))
        sel_t = _topk_cols_mask(gate_t, valid_f, blk_r.astype(F32), MOBA_TOPK, nblk)
        bias = jnp.where(sel_t > 0.0, 0.0, NEG).T
        qa = jnp.concatenate(
            [jnp.concatenate([q2[:, :HALF] * cq, bias[:tq], pad], axis=1),
             jnp.concatenate([bias[tq:], pad, q2[:, HALF:] * cq], axis=1)], axis=0)
        qa_ref[hp] = _mx(qa)
        block_update(hp, 0, own0, -LANES, causal)

    def body(i, _):
        for par in range(N_STATES):
            n = N_STATES * i + par
            k0 = pl.multiple_of(jnp.minimum(n, nblk - 1) * tq, tq)
            for hp in range(nhp):
                block_update(hp, par, k0, n, None)
        return 0

    lax.fori_loop(0, (qi + N_STATES - 1) // N_STATES, body, 0)
    for hp in range(nhp):
        sl = slice(hp * LANES, (hp + 1) * LANES)
        acc = _merge_states(m_ref.at[hp], acc_ref.at[hp])
        a0 = acc[0:tq, :]
        a1 = acc[tq:2 * tq, :]
        o = jnp.where(lo, a0 / _swap_halves(a0), a1 / _swap_halves(a1))
        o_ref[:, sl] = o * _silu(g_ref[:, sl])


def _moba_prompt(pr, nb, t):
    nblk = t // MOBA_BLOCK
    tq = MOBA_BLOCK
    wd = MOBA_H * MOBA_HD
    nhp = MOBA_PAIRS_PER_STEP
    ws = nhp * LANES
    qspec = pl.BlockSpec((tq, ws), lambda b, hg, qi: (b * nblk + qi, hg))
    kspec = pl.BlockSpec((t, ws), lambda b, hg, qi: (b, hg))
    return pl.pallas_call(
        functools.partial(_moba_p_kernel, nblk=nblk),
        grid=(nb, wd // ws, nblk),
        in_specs=[qspec, kspec, kspec, pl.BlockSpec((nblk, 1, ws), lambda b, hg, qi: (b, 0, hg)), qspec],
        out_specs=qspec,
        out_shape=jax.ShapeDtypeStruct((nb * t, wd), F32),
        scratch_shapes=[pltpu.VMEM((nhp, 2 * tq, LANES), MXU_DTYPE),
                        pltpu.VMEM((nhp, N_STATES, 2 * tq, LANES), F32),
                        pltpu.VMEM((nhp, N_STATES, 2 * tq, LANES), F32)],
        compiler_params=_params(("parallel", "parallel", "arbitrary")),
        name="moba_prompt",
    )(pr["mq"], pr["mk16"], pr["mv16"], pr["kbar"], pr["mg"])


def _select_topk_bias_t(key_ref, bias_ref, nch, valid_fn, n, topk, idx_bits):
    rowk = lax.broadcasted_iota(jnp.int32, (n, n), 0)
    kf = float(topk)
    sub = 8

    def count(pred_fn):
        def cb(i, acc):
            for par in range(2):
                c = 2 * i + par
                hit = jnp.where(pred_fn(c, key_ref[c]), 1.0, 0.0)
                acc = acc + jnp.sum(hit.reshape(n // sub, sub, n), axis=0)
            return acc
        acc = lax.fori_loop(0, (nch + 1) // 2, cb, jnp.zeros((sub, n), F32))
        return jnp.sum(acc, axis=0, keepdims=True)

    def bit_body(i, carry):
        thr, cnt = carry
        cand = thr + lax.shift_left(jnp.int32(1), 31 - i)
        c2 = count(lambda c, kc: kc >= cand)
        ok = c2 >= kf
        return jnp.where(ok, cand, thr), jnp.where(ok, c2, cnt)

    total = (nch * n + jnp.zeros((1, n), jnp.int32)).astype(F32)
    thr, cnt = lax.fori_loop(0, 32, bit_body, (jnp.full((1, n), INT_MIN, jnp.int32), total))

    def tie_search():
        need = kf - count(lambda c, kc: kc > thr)

        def idx_body(i, x):
            cand = x + lax.shift_left(jnp.int32(1), idx_bits - 1 - i)
            c2 = count(lambda c, kc: (kc == thr) & (c * n + rowk < cand))
            return jnp.where(c2 < need, cand, x)

        return lax.fori_loop(0, idx_bits, idx_body, jnp.zeros((1, n), jnp.int32)) + 1

    jstar = lax.cond(jnp.max(cnt) > kf, tie_search, lambda: jnp.full((1, n), INT_MAX, jnp.int32))

    def wb(c, _):
        kc = key_ref[c]
        take = (kc > thr) | ((kc == thr) & (c * n + rowk < jstar))
        bias_ref[c] = jnp.where(take & valid_fn(c), 0.0, NEG).T
        return 0

    lax.fori_loop(0, nch, wb, 0)


def _dsa_p_kernel(q_ref, iq_ref, iwt_ref, k_ref, v_ref, ki_ref, g_ref, o_ref,
                  key_ref, bias_ref, qs_ref, m_ref, acc_ref, *, topk, idx_bits):
    qi = pl.program_id(1)
    tq = q_ref.shape[0]
    nch = qi + 1
    grp = DSA_H // DSA_KVH
    lane = lax.broadcasted_iota(jnp.int32, (tq, LANES), 1)
    lo = lane < HALF

    def valid_t(c):
        rowk = lax.broadcasted_iota(jnp.int32, (tq, tq), 0)
        colq = lax.broadcasted_iota(jnp.int32, (tq, tq), 1)
        return (rowk <= colq) | (c < qi)

    iq = iq_ref[...]
    wrow = (iwt_ref[...] * (IDX_H ** -0.5)) * (IDX_D ** -0.5)
    iq_all = jnp.concatenate([_mx(iq[:, h * IDX_D:(h + 1) * IDX_D]) for h in range(IDX_H)], axis=0)

    def score_body(i, _):
        for par in range(2):
            c = 2 * i + par
            cc = jnp.minimum(c, qi)
            kic = _mx(ki_ref[pl.ds(pl.multiple_of(cc * tq, tq), tq), :])
            rel = _dot_nt(kic, iq_all)
            sc = jnp.zeros((tq, tq), F32)
            for h in range(IDX_H):
                sc = sc + jnp.maximum(rel[:, h * tq:(h + 1) * tq], 0.0) * wrow[h:h + 1, :]
            key = _sort_key(jnp.where(valid_t(c), sc, -jnp.inf))
            key_ref[c] = jnp.where(c < nch, key, INT_MIN)
        return 0

    lax.fori_loop(0, (nch + 1) // 2, score_body, 0)
    _select_topk_bias_t(key_ref, bias_ref, nch, valid_t, tq, topk, idx_bits)

    q = q_ref[...]
    cq = DSA_HD ** -0.5 * LOG2E
    for g in range(DSA_H):
        kvh, j = g // grp, g % grp
        slab = q[:, (g // 2) * LANES:(g // 2 + 1) * LANES]
        if g % 2 != kvh:
            slab = _swap_halves(slab)
        in_kv = lo if kvh == 0 else jnp.logical_not(lo)
        qs_ref[kvh, j * tq:(j + 1) * tq, :] = _mx(jnp.where(in_kv, slab, 0.0) * cq)
    m_ref[...] = jnp.full(m_ref.shape, NEG, F32)
    acc_ref[...] = jnp.zeros(acc_ref.shape, F32)

    def att_body(i, _):
        for par in range(N_STATES):
            c = N_STATES * i + par
            cc = jnp.minimum(c, qi)
            k0 = pl.multiple_of(cc * tq, tq)
            kb = _mx(k_ref[pl.ds(k0, tq), :])
            vb = v_ref[pl.ds(k0, tq), :]
            bias = bias_ref[cc] + jnp.where(c < nch, 0.0, NEG)
            bias4 = jnp.concatenate([bias] * grp, axis=0)
            for kvh in range(DSA_KVH):
                va = _mx(jnp.where(lo, vb, 1.0) if kvh == 0 else jnp.where(lo, 1.0, vb))
                s = _dot_nt(qs_ref[kvh], kb) + bias4
                _softmax_update(s, m_ref.at[par, kvh], acc_ref.at[par, kvh],
                                lambda p, va=va: _dot(p, va))
        return 0

    lax.fori_loop(0, (nch + N_STATES - 1) // N_STATES, att_body, 0)

    g_all = g_ref[...]
    acc = [_merge_states(m_ref.at[:, kvh], acc_ref.at[:, kvh]) for kvh in range(DSA_KVH)]
    for pair in range(DSA_H // 2):
        res = []
        for jj in range(2):
            g = 2 * pair + jj
            kvh, j = g // grp, g % grp
            a = acc[kvh][j * tq:(j + 1) * tq, :]
            o = a / _swap_halves(a)
            res.append(o if jj == kvh else _swap_halves(o))
        sl = slice(pair * LANES, (pair + 1) * LANES)
        o_ref[:, sl] = jnp.where(lo, res[0], res[1]) * _silu(g_all[:, sl])


def _dsa_prompt(pr, nb, t):
    tq = 256
    nq = t // tq
    topk = min(DSA_TOPK, t // 4)
    idx_bits = max(1, math.ceil(math.log2(t)))
    grp = DSA_H // DSA_KVH
    tok = lambda wd: pl.BlockSpec((tq, wd), lambda b, qi: (b * nq + qi, 0))
    seq = lambda wd: pl.BlockSpec((t, wd), lambda b, qi: (b, 0))
    iwt = pr["iw"].T
    return pl.pallas_call(
        functools.partial(_dsa_p_kernel, topk=topk, idx_bits=idx_bits),
        grid=(nb, nq),
        in_specs=[tok(512), tok(512), pl.BlockSpec((IDX_H, tq), lambda b, qi: (0, b * nq + qi)),
                  seq(LANES), seq(LANES), seq(IDX_D), tok(512)],
        out_specs=tok(512),
        out_shape=jax.ShapeDtypeStruct((nb * t, DSA_H * DSA_HD), F32),
        scratch_shapes=[pltpu.VMEM((nq, tq, tq), jnp.int32), pltpu.VMEM((nq, tq, tq), F32),
                        pltpu.VMEM((DSA_KVH, grp * tq, LANES), MXU_DTYPE),
                        pltpu.VMEM((N_STATES, DSA_KVH, grp * tq, LANES), F32),
                        pltpu.VMEM((N_STATES, DSA_KVH, grp * tq, LANES), F32)],
        compiler_params=_params(("parallel", "arbitrary")),
        name="dsa_prompt",
    )(pr["dq"], pr["iq"], iwt, pr["dk"], pr["dv"], pr["ik"], pr["dg"])


def _memkv_kernel(x_ref, g_ref, w_ref, k_ref, v_ref):
    h = _mx(_rmsnorm(x_ref[...], g_ref[...]))
    z = _dot(h, w_ref[...])
    half = MEM_H * MEM_HD
    k_ref[...] = z[:, :half]
    v_ref[...] = z[:, half:]


def _mem_kv(mem2, g, w):
    n = mem2.shape[0]
    tm = 256
    half = MEM_H * MEM_HD
    tok = lambda wd: pl.BlockSpec((tm, wd), lambda i: (i, 0))
    return pl.pallas_call(
        _memkv_kernel,
        grid=(n // tm,),
        in_specs=[tok(D_MODEL), _const_spec((1, D_MODEL)), _const_spec((D_MODEL, 2 * half))],
        out_specs=[tok(half), tok(half)],
        out_shape=[jax.ShapeDtypeStruct((n, half), F32)] * 2,
        compiler_params=_params(("parallel",)),
        name="mem_kv",
    )(mem2, g, w)


def _mem_kernel(q_ref, k_ref, v_ref, g_ref, o_ref):
    q, k, v = q_ref[...], k_ref[...], v_ref[...]
    outs = []
    for h in range(MEM_H):
        sl = slice(h * MEM_HD, (h + 1) * MEM_HD)
        s = _dot_nt(_mx(q[:, sl]), _mx(k[:, sl])) * (MEM_HD ** -0.5)
        m = jnp.max(s, axis=-1, keepdims=True)
        p = jnp.exp(s - m)
        p = p / jnp.sum(p, axis=-1, keepdims=True)
        outs.append(_dot(_mx(p), _mx(v[:, sl])))
    o_ref[...] = jnp.concatenate(outs, axis=1) * _silu(g_ref[...])


def _mem_attend(q, mk, mv, g, nb, t, tq):
    nq = t // tq
    ntok = mk.shape[0] // nb
    tok = pl.BlockSpec((tq, 512), lambda b, i: (b * nq + i, 0))
    kv = pl.BlockSpec((ntok, 512), lambda b, i: (b, 0))
    return pl.pallas_call(
        _mem_kernel,
        grid=(nb, nq),
        in_specs=[tok, kv, kv, tok],
        out_specs=tok,
        out_shape=jax.ShapeDtypeStruct((nb * t, 512), F32),
        compiler_params=_params(("parallel", "arbitrary")),
        name="mem_attend",
    )(q, mk, mv, g)


def _merge_kernel(x_ref, g_ref, wbg_ref, o0, o1, o2, o3, wb_ref, wo_ref, gf_ref, y_ref, *, final):
    x = x_ref[...]
    h = _mx(_rmsnorm(x, g_ref[...]))
    merged = jnp.zeros(x.shape, F32)
    for b, o in enumerate((o0, o1, o2, o3)):
        gate = _sigmoid(_dot(h, wbg_ref[:, b * D_MODEL:(b + 1) * D_MODEL]))
        merged = merged + gate * _dot(_mx(o[...]), wb_ref[b])
    y = x + _dot(_mx(merged), wo_ref[...])
    y_ref[...] = _rmsnorm(y, gf_ref[...]) if final else y


def _merge(x2, g, wbg, outs, wb, wo, gf, final, tm):
    n = x2.shape[0]
    tok = lambda wd: pl.BlockSpec((tm, wd), lambda i: (i, 0))
    return pl.pallas_call(
        functools.partial(_merge_kernel, final=final),
        grid=(n // tm,),
        in_specs=[tok(D_MODEL), _const_spec((1, D_MODEL)), _const_spec((D_MODEL, W_BG)),
                  tok(BR_W), tok(BR_W), tok(BR_W), tok(BR_W),
                  _const_spec((N_BRANCH, BR_W, D_MODEL)), _const_spec((D_MODEL, D_MODEL)),
                  _const_spec((1, D_MODEL))],
        out_specs=tok(D_MODEL),
        out_shape=jax.ShapeDtypeStruct((n, D_MODEL), F32),
        compiler_params=_params(("parallel",)),
        name="merge",
    )(x2, g, wbg, *outs, wb, wo, gf)


def _pages_t(cache):
    nd = cache.ndim
    perm = (0, 1) + tuple(range(3, nd)) + (2,)
    t = jnp.transpose(cache, perm)
    return t.reshape(t.shape[0], t.shape[1], -1, PAGE)


def _page_copy(cache_ref, buf_ref, sem_ref, slot, layer, page, i):
    return pltpu.make_async_copy(cache_ref.at[layer, page],
                                 buf_ref.at[slot, :, i * PAGE:(i + 1) * PAGE], sem_ref.at[slot])


def _start_pages(cache_ref, buf_ref, sem_ref, slot, layer, pt_ref, b, j, pps):
    for i in range(pps):
        _page_copy(cache_ref, buf_ref, sem_ref, slot, layer, pt_ref[b, j * pps + i], i).start()


def _wait_pages(cache_ref, buf_ref, sem_ref, slot, layer, pps):
    for i in range(pps):
        _page_copy(cache_ref, buf_ref, sem_ref, slot, layer, 0, i).wait()


def _sample_step(nj):
    b, ph, j = pl.program_id(0), pl.program_id(1), pl.program_id(2)
    return (b * 2 + ph) * nj + j, pl.num_programs(0) * 2 * nj


def _prefetch_pages(t, total, issue):
    @pl.when(t == 0)
    def _():
        issue(t)

    @pl.when(t + 1 < total)
    def _():
        issue(t + 1)


def _pad_rows(x, rows):
    return jnp.concatenate([x, jnp.zeros((rows - x.shape[0], x.shape[1]), x.dtype)], axis=0)


def _moba_s_kernel(pt_ref, qbd_ref, kn_ref, vn_ref, g_ref, ck_ref, cv_ref, o_ref,
                   s_ref, rl_ref, acc_ref, buf_ref, sem_ref, *, pps, nj, layer):
    ph, j = pl.program_id(1), pl.program_id(2)
    nq = kn_ref.shape[0]
    nr = MOBA_H * nq
    wdt = pps * PAGE
    bps = wdt // MOBA_BLOCK
    nblk = nj * bps
    c2 = MOBA_HD ** -0.5 * LOG2E
    t, total = _sample_step(nj)
    slot = t % 2

    def issue(tt):
        bb, pp, jj = tt // (2 * nj), (tt // nj) % 2, tt % nj
        for phase, cache_ref in enumerate((ck_ref, cv_ref)):
            @pl.when(pp == phase)
            def _(cache_ref=cache_ref):
                _start_pages(cache_ref, buf_ref, sem_ref, tt % 2, layer, pt_ref, bb, jj, pps)

    _prefetch_pages(t, total, issue)
    _wait_pages(ck_ref, buf_ref, sem_ref, slot, layer, pps)

    @pl.when(ph == 0)
    def _():
        s_ref[j] = _dot(_mx(qbd_ref[0]), _mx(buf_ref[slot]))

    @pl.when((ph == 1) & (j == 0))
    def _():
        blane = lax.broadcasted_iota(jnp.int32, (nr, nblk), 1)
        gate = jnp.zeros((nr, nblk), F32)
        bmax = jnp.full((nr, nblk), NEG, F32)
        for jj in range(nj):
            x = s_ref[jj]
            for b in range(bps):
                xb = x[:, b * MOBA_BLOCK:(b + 1) * MOBA_BLOCK]
                here = blane == jj * bps + b
                gate = jnp.where(here, jnp.sum(xb, axis=-1, keepdims=True) * (1.0 / MOBA_BLOCK), gate)
                bmax = jnp.where(here, jnp.max(xb, axis=-1, keepdims=True), bmax)
        sel = _topk_rows_mask(gate, jnp.ones((nr, nblk), F32), blane.astype(F32),
                              min(MOBA_TOPK, nblk), nblk)

        lane = lax.broadcasted_iota(jnp.int32, (nr, LANES), 1)
        qrow = lax.broadcasted_iota(jnp.int32, (nr, LANES), 0) % nq
        s_own = jnp.where(lane <= qrow, _dot_nt(_mx(qbd_ref[0]), _mx(_pad_rows(kn_ref[...], LANES))), NEG)
        m = jnp.maximum(jnp.max(jnp.where(sel > 0, bmax, NEG), axis=-1, keepdims=True),
                        jnp.max(s_own, axis=-1, keepdims=True))
        p_own = jnp.exp2((s_own - m) * c2)
        l = jnp.sum(p_own, axis=-1, keepdims=True)
        for jj in range(nj):
            x = s_ref[jj]
            parts = []
            for b in range(bps):
                on = jnp.sum(jnp.where(blane == jj * bps + b, sel, 0.0), axis=-1, keepdims=True) > 0.0
                xb = x[:, b * MOBA_BLOCK:(b + 1) * MOBA_BLOCK]
                parts.append(jnp.where(on, jnp.exp2((xb - m) * c2), 0.0))
            p = jnp.concatenate(parts, axis=1)
            l = l + jnp.sum(p, axis=-1, keepdims=True)
            s_ref[jj] = p
        rl = 1.0 / l
        rl_ref[...] = jnp.broadcast_to(rl, rl_ref.shape)
        acc_ref[...] = _dot(_mx(p_own * rl), _mx(_pad_rows(vn_ref[...], LANES)))

    @pl.when(ph == 1)
    def _():
        p = _mx(s_ref[j] * rl_ref[:, 0:1])
        acc_ref[...] += _dot_nt(p, _mx(buf_ref[slot]))

    @pl.when((ph == 1) & (j == nj - 1))
    def _():
        acc = acc_ref[...]
        parts = [acc[h * nq:(h + 1) * nq, h * MOBA_HD:(h + 1) * MOBA_HD] for h in range(MOBA_H)]
        o_ref[...] = jnp.concatenate(parts, axis=1) * _silu(g_ref[...])


def _moba_sample(ps, ck_t, cv_t, layer, page_table, nb, nq):
    n_pages = page_table.shape[1]
    pps = min(SAMPLE_PAGES_PER_STEP, n_pages)
    nj = n_pages // pps
    wd = MOBA_H * MOBA_HD
    nr = MOBA_H * nq
    q4 = ps["mq"].reshape(nb, nq, MOBA_H, MOBA_HD).transpose(0, 2, 1, 3)
    eye = jnp.eye(MOBA_H, dtype=F32)
    qbd = (q4[:, :, :, None, :] * eye[None, :, None, :, None]).reshape(nb, nr, wd)
    tok = pl.BlockSpec((nq, wd), lambda b, ph, j, pt: (b, 0))
    hbm = pl.BlockSpec(memory_space=pl.ANY)
    grid_spec = pltpu.PrefetchScalarGridSpec(
        num_scalar_prefetch=1,
        grid=(nb, 2, nj),
        in_specs=[pl.BlockSpec((1, nr, wd), lambda b, ph, j, pt: (b, 0, 0)), tok, tok, tok, hbm, hbm],
        out_specs=tok,
        scratch_shapes=[pltpu.VMEM((nj, nr, pps * PAGE), F32),
                        pltpu.VMEM((nr, LANES), F32),
                        pltpu.VMEM((nr, wd), F32),
                        pltpu.VMEM((2, wd, pps * PAGE), F32),
                        pltpu.SemaphoreType.DMA((2,))],
    )
    return pl.pallas_call(
        functools.partial(_moba_s_kernel, pps=pps, nj=nj, layer=layer),
        grid_spec=grid_spec,
        out_shape=jax.ShapeDtypeStruct((nb * nq, wd), F32),
        compiler_params=_params(("arbitrary", "arbitrary", "arbitrary")),
        name="moba_sample",
    )(page_table, qbd, ps["mk"], ps["mv"], ps["mg"], ck_t, cv_t)


def _dsa_s_kernel(pt_ref, qi_ref, wi_ref, qz_ref, ikn_ref, kn_ref, vn_ref, g_ref,
                  ci_ref, ck_ref, cv_ref, o_ref, key_ref, bias_ref, m_ref, l_ref, acc_ref,
                  ibuf_ref, kbuf_ref, vbuf_ref, isem_ref, ksem_ref, vsem_ref,
                  *, pps, nj, topk, idx_bits, layer):
    ph, j = pl.program_id(1), pl.program_id(2)
    nq = ikn_ref.shape[0]
    wdt = pps * PAGE
    scale = DSA_HD ** -0.5
    lane = lax.broadcasted_iota(jnp.int32, (nq, wdt), 1)
    qrow = lax.broadcasted_iota(jnp.int32, (nq, wdt), 0)
    new_valid = lane <= qrow
    t, total = _sample_step(nj)
    slot = t % 2

    def issue(tt):
        bb, pp, jj = tt // (2 * nj), (tt // nj) % 2, tt % nj

        @pl.when(pp == 0)
        def _():
            _start_pages(ci_ref, ibuf_ref, isem_ref, tt % 2, layer, pt_ref, bb, jj, pps)

        @pl.when(pp == 1)
        def _():
            _start_pages(ck_ref, kbuf_ref, ksem_ref, tt % 2, layer, pt_ref, bb, jj, pps)
            _start_pages(cv_ref, vbuf_ref, vsem_ref, tt % 2, layer, pt_ref, bb, jj, pps)

    _prefetch_pages(t, total, issue)

    def idx_scores(rel):
        rel = jnp.maximum(rel * (IDX_D ** -0.5), 0.0) * wi_ref[0]
        return jnp.sum(rel.reshape(IDX_H, nq, rel.shape[1]), axis=0)

    @pl.when(ph == 0)
    def _():
        _wait_pages(ci_ref, ibuf_ref, isem_ref, slot, layer, pps)
        key_ref[j] = _sort_key(idx_scores(_dot(_mx(qi_ref[0]), _mx(ibuf_ref[slot]))))

    def attend(s, pv_fn):
        m = m_ref[...]
        m_new = jnp.maximum(m, jnp.max(s, axis=-1, keepdims=True))
        alpha = jnp.exp(m - m_new)
        p = jnp.exp(s - m_new)
        l_ref[...] = alpha * l_ref[...] + jnp.sum(p, axis=-1, keepdims=True)
        acc_ref[...] = alpha * acc_ref[...] + pv_fn(_mx(p))
        m_ref[...] = m_new

    @pl.when((ph == 1) & (j == 0))
    def _():
        ik_new = _mx(_pad_rows(ikn_ref[...], wdt))
        s_new = idx_scores(_dot_nt(_mx(qi_ref[0]), ik_new))
        key_ref[nj] = _sort_key(jnp.where(new_valid, s_new, -jnp.inf))

        def valid(c, rows):
            return new_valid | (c < nj)
        _select_topk_bias(key_ref, bias_ref, nj + 1, valid, slice(0, nq), wdt, topk, idx_bits)

        m_ref[...] = jnp.full(m_ref.shape, NEG, F32)
        l_ref[...] = jnp.zeros(l_ref.shape, F32)
        acc_ref[...] = jnp.zeros(acc_ref.shape, F32)
        k_new = _mx(_pad_rows(kn_ref[...], wdt))
        v_new = _mx(_pad_rows(vn_ref[...], wdt))
        s = _dot_nt(_mx(qz_ref[0]), k_new) * scale + jnp.concatenate([bias_ref[nj]] * DSA_H, axis=0)
        attend(s, lambda p: _dot(p, v_new))

    @pl.when(ph == 1)
    def _():
        _wait_pages(ck_ref, kbuf_ref, ksem_ref, slot, layer, pps)
        _wait_pages(cv_ref, vbuf_ref, vsem_ref, slot, layer, pps)
        s = (_dot(_mx(qz_ref[0]), _mx(kbuf_ref[slot])) * scale
             + jnp.concatenate([bias_ref[j]] * DSA_H, axis=0))
        attend(s, lambda p: _dot_nt(p, _mx(vbuf_ref[slot])))

    @pl.when((ph == 1) & (j == nj - 1))
    def _():
        o = acc_ref[...] / l_ref[...]
        grp = DSA_H // DSA_KVH
        parts = []
        for g in range(DSA_H):
            kvh = g // grp
            parts.append(o[g * nq:(g + 1) * nq, kvh * DSA_HD:(kvh + 1) * DSA_HD])
        o_ref[...] = jnp.concatenate(parts, axis=1) * _silu(g_ref[...])


def _dsa_sample(ps, ck_t, cv_t, ci_t, layer, page_table, nb, nq):
    n_pages = page_table.shape[1]
    pps = min(SAMPLE_PAGES_PER_STEP, n_pages)
    nj = n_pages // pps
    n_keys = n_pages * PAGE + nq
    topk = min(DSA_TOPK, n_keys // 4)
    idx_bits = max(1, math.ceil(math.log2((nj + 1) * pps * PAGE)))
    grp = DSA_H // DSA_KVH
    qi = ps["iq"].reshape(nb, nq, IDX_H, IDX_D).transpose(0, 2, 1, 3).reshape(nb, IDX_H * nq, IDX_D)
    wi = (ps["iw"].reshape(nb, nq, IDX_H).transpose(0, 2, 1) * (IDX_H ** -0.5)).reshape(nb, IDX_H * nq, 1)
    q4 = ps["dq"].reshape(nb, nq, DSA_H, DSA_HD).transpose(0, 2, 1, 3)
    kv_of = (jnp.arange(DSA_H) // grp)[:, None] == jnp.arange(DSA_KVH)[None, :]
    qz = (q4[:, :, :, None, :] * kv_of.astype(F32)[None, :, None, :, None]).reshape(
        nb, DSA_H * nq, DSA_KVH * DSA_HD)
    per_b = lambda r, c: pl.BlockSpec((1, r, c), lambda b, ph, j, pt: (b, 0, 0))
    tok = lambda wd: pl.BlockSpec((nq, wd), lambda b, ph, j, pt: (b, 0))
    hbm = pl.BlockSpec(memory_space=pl.ANY)
    grid_spec = pltpu.PrefetchScalarGridSpec(
        num_scalar_prefetch=1,
        grid=(nb, 2, nj),
        in_specs=[per_b(IDX_H * nq, IDX_D), per_b(IDX_H * nq, 1), per_b(DSA_H * nq, LANES),
                  tok(IDX_D), tok(LANES), tok(LANES), tok(512), hbm, hbm, hbm],
        out_specs=tok(512),
        scratch_shapes=[pltpu.VMEM((nj + 1, nq, pps * PAGE), jnp.int32),
                        pltpu.VMEM((nj + 1, nq, pps * PAGE), F32),
                        pltpu.VMEM((DSA_H * nq, 1), F32),
                        pltpu.VMEM((DSA_H * nq, 1), F32),
                        pltpu.VMEM((DSA_H * nq, LANES), F32),
                        pltpu.VMEM((2, IDX_D, pps * PAGE), F32),
                        pltpu.VMEM((2, LANES, pps * PAGE), F32),
                        pltpu.VMEM((2, LANES, pps * PAGE), F32),
                        pltpu.SemaphoreType.DMA((2,)),
                        pltpu.SemaphoreType.DMA((2,)),
                        pltpu.SemaphoreType.DMA((2,))],
    )
    return pl.pallas_call(
        functools.partial(_dsa_s_kernel, pps=pps, nj=nj, topk=topk, idx_bits=idx_bits, layer=layer),
        grid_spec=grid_spec,
        out_shape=jax.ShapeDtypeStruct((nb * nq, 512), F32),
        compiler_params=_params(("arbitrary", "arbitrary", "arbitrary")),
        name="dsa_sample",
    )(page_table, qi, wi, qz, ps["ik"], ps["dk"], ps["dv"], ps["dg"], ci_t, ck_t, cv_t)


def _layer_weights(w_in_l):
    cuts = np.cumsum([0, W_RET, W_MOBA, W_DSA, W_IDX, W_MEM, W_BG])
    seg = [w_in_l[:, int(a):int(b)] for a, b in zip(cuts[:-1], cuts[1:])]
    idx = jnp.pad(seg[3], ((0, 0), (0, W_IDX_PAD - W_IDX)))
    return dict(ret=_mx(seg[0]), moba=_mx(seg[1]), dsa=_mx(seg[2]), idx=_mx(idx),
                mem=_mx(seg[4]), bg=_mx(seg[5]))


def kernel(x_prompt, x_sample, cache_moba_k, cache_moba_v, cache_dsa_k, cache_dsa_v, cache_dsa_idx_k,
           cache_mem_k, cache_mem_v, state_ret, page_table, mem_prompt, g_norm, w_in, g_ret, g_mem,
           w_mem_kv, w_branch, w_out, g_final):
    nb_p, t_p, d = x_prompt.shape
    nb_s, t_s, _ = x_sample.shape
    depth = w_in.shape[0]
    past = page_table.shape[1] * PAGE
    pos_p = jnp.arange(t_p, dtype=jnp.int32)
    pos_s = past + jnp.arange(t_s, dtype=jnp.int32)
    tm_p = 256
    n_s = nb_s * t_s
    tab_p = (_rope_tables(pos_p, RET_DK, RET_THETA, RET_DK),
             _rope_tables(pos_p, MOBA_HD // ROT_DIV, ROPE_THETA, MOBA_HD))
    tab_s = tuple(jnp.tile(_rope_tables(pos_s, r, th, 64), (1, nb_s, 1))
                  for r, th in ((RET_DK, RET_THETA), (MOBA_HD // ROT_DIV, ROPE_THETA)))
    s0_p = jnp.zeros((nb_p, RET_H, RET_DK, RET_DV), F32)
    mem2 = mem_prompt.reshape(nb_p * mem_prompt.shape[1], d)
    gf = g_final.reshape(1, d)
    moba_kt, moba_vt = _pages_t(cache_moba_k), _pages_t(cache_moba_v)
    dsa_kt, dsa_vt, dsa_it = _pages_t(cache_dsa_k), _pages_t(cache_dsa_v), _pages_t(cache_dsa_idx_k)

    xp = x_prompt.reshape(nb_p * t_p, d)
    xs = x_sample.reshape(n_s, d)
    acc = {k: [] for k in ("pmk", "pmv", "pdk", "pdv", "pdi", "pret", "pek", "pev",
                           "smk", "smv", "sdk", "sdv", "sdi", "sret")}
    for l in range(depth):
        w = _layer_weights(w_in[l])
        gn = g_norm[l].reshape(1, d)
        gr = g_ret[l].reshape(1, RET_H * RET_DV)
        wb = _mx(w_branch[l])
        wo = _mx(w_out[l])
        final = l == depth - 1

        pr = _project(xp, gn, w, tab_p[0], tab_p[1], tm_p)
        o_ret, st_p = _retention(pr, s0_p, gr, nb_p, t_p, RET_CHUNK)
        o_moba = _moba_prompt(pr, nb_p, t_p)
        o_dsa = _dsa_prompt(pr, nb_p, t_p)
        mk, mv = _mem_kv(mem2, g_mem[l].reshape(1, d), _mx(w_mem_kv[l]))
        o_mem = _mem_attend(pr["eq"], mk, mv, pr["eg"], nb_p, t_p, 256)
        xp = _merge(xp, gn, w["bg"], (o_ret, o_moba, o_dsa, o_mem), wb, wo, gf, final, tm_p)
        acc["pmk"].append(pr["mk"]); acc["pmv"].append(pr["mv"])
        acc["pdk"].append(pr["dk"]); acc["pdv"].append(pr["dv"]); acc["pdi"].append(pr["ik"])
        acc["pret"].append(st_p); acc["pek"].append(mk); acc["pev"].append(mv)

        ps = _project(xs, gn, w, tab_s[0], tab_s[1], n_s)
        o_ret_s, st_s = _retention(ps, state_ret[l], gr, nb_s, t_s, t_s)
        o_moba_s = _moba_sample(ps, moba_kt, moba_vt, l, page_table, nb_s, t_s)
        o_dsa_s = _dsa_sample(ps, dsa_kt, dsa_vt, dsa_it, l, page_table, nb_s, t_s)
        mk_s = cache_mem_k[l].reshape(nb_s * cache_mem_k.shape[2], MEM_H * MEM_HD)
        mv_s = cache_mem_v[l].reshape(nb_s * cache_mem_v.shape[2], MEM_H * MEM_HD)
        o_mem_s = _mem_attend(ps["eq"], mk_s, mv_s, ps["eg"], nb_s, t_s, t_s)
        xs = _merge(xs, gn, w["bg"], (o_ret_s, o_moba_s, o_dsa_s, o_mem_s), wb, wo, gf, final, n_s)
        acc["smk"].append(ps["mk"]); acc["smv"].append(ps["mv"])
        acc["sdk"].append(ps["dk"]); acc["sdv"].append(ps["dv"]); acc["sdi"].append(ps["ik"])
        acc["sret"].append(st_s)

    st = lambda k, shape: jnp.stack(acc[k]).reshape((depth,) + shape)
    ntok = mem_prompt.shape[1]
    return (xp.reshape(nb_p, t_p, d), xs.reshape(nb_s, t_s, d),
            st("pmk", (nb_p, t_p, MOBA_H, MOBA_HD)), st("pmv", (nb_p, t_p, MOBA_H, MOBA_HD)),
            st("pdk", (nb_p, t_p, DSA_KVH, DSA_HD)), st("pdv", (nb_p, t_p, DSA_KVH, DSA_HD)),
            st("pdi", (nb_p, t_p, IDX_D)),
            st("pret", (nb_p, RET_H, RET_DK, RET_DV)),
            st("pek", (nb_p, ntok, MEM_H, MEM_HD)), st("pev", (nb_p, ntok, MEM_H, MEM_HD)),
            st("smk", (nb_s, t_s, MOBA_H, MOBA_HD)), st("smv", (nb_s, t_s, MOBA_H, MOBA_HD)),
            st("sdk", (nb_s, t_s, DSA_KVH, DSA_HD)), st("sdv", (nb_s, t_s, DSA_KVH, DSA_HD)),
            st("sdi", (nb_s, t_s, IDX_D)),
            st("sret", (nb_s, RET_H, RET_DK, RET_DV)))
```

```python
import functools
import math

import jax
import jax.numpy as jnp
import numpy as np
from jax import lax
from jax.experimental import pallas as pl
from jax.experimental.pallas import tpu as pltpu

F32 = jnp.float32
MXU_DTYPE = jnp.bfloat16

D_MODEL = 1024
PAGE = 128
RET_H, RET_DK, RET_DV = 4, 64, 128
RET_CHUNK = 128
RET_THETA = 10000.0
MOBA_H, MOBA_HD, MOBA_BLOCK, MOBA_TOPK = 8, 64, 256, 3
DSA_H, DSA_KVH, DSA_HD = 8, 2, 64
IDX_H, IDX_D = 8, 64
DSA_TOPK = 256
MEM_H, MEM_HD = 4, 128
BR_W = 512
N_BRANCH = 4
ROPE_THETA = 500000.0
ROT_DIV = 4
EPS = 1e-6
LANES = 128
HALF = LANES // 2
NEG = -1e30
INT_MIN = -2147483648
INT_MAX = 2147483647
LOG2E = math.log2(math.e)
VMEM_LIMIT = 56 * 1024 * 1024
SAMPLE_PAGES_PER_STEP = 32
RET_SEQS_PER_STEP = 8
MOBA_PAIRS_PER_STEP = 4
N_STATES = 2

W_RET = 2 * RET_H * RET_DK + RET_H * RET_DV + BR_W
W_MOBA = 3 * MOBA_H * MOBA_HD + BR_W
W_DSA = DSA_H * DSA_HD + 2 * DSA_KVH * DSA_HD + BR_W
W_IDX = IDX_H * IDX_D + IDX_D + IDX_H
W_IDX_PAD = 640
W_MEM = MEM_H * MEM_HD + BR_W
W_BG = N_BRANCH * D_MODEL


def _dot(a, b):
    return jnp.dot(a, b, preferred_element_type=F32)


def _dot_nt(a, b):
    return lax.dot_general(a, b, (((1,), (1,)), ((), ())), preferred_element_type=F32)


def _dot_tn(a, b):
    return lax.dot_general(a, b, (((0,), (0,)), ((), ())), preferred_element_type=F32)


def _mx(a):
    return a.astype(MXU_DTYPE)


def _silu(x):
    return x * (1.0 / (1.0 + jnp.exp(-x)))


def _sigmoid(x):
    return 1.0 / (1.0 + jnp.exp(-x))


def _rmsnorm(x, g):
    return x * lax.rsqrt(jnp.mean(x * x, axis=-1, keepdims=True) + EPS) * g


def _params(sem):
    return pltpu.CompilerParams(dimension_semantics=sem, vmem_limit_bytes=VMEM_LIMIT)


def _const_spec(shape):
    nd = len(shape)
    return pl.BlockSpec(shape, lambda *_: (0,) * nd)


def _swap_halves(x):
    return pltpu.roll(x, HALF, 1)


def _rope_tables(pos, rot_dim, theta, hd):
    half = rot_dim // 2
    inv = jnp.power(theta, -jnp.arange(half, dtype=F32) * 2.0 / rot_dim)
    ang = pos.astype(F32)[:, None] * inv[None, :]
    cos, sin = jnp.cos(ang), jnp.sin(ang)
    t = pos.shape[0]
    z_half = jnp.zeros((t, half), F32)
    rest0 = jnp.zeros((t, hd - rot_dim), F32)
    rest1 = jnp.ones((t, hd - rot_dim), F32)
    c = jnp.concatenate([cos, cos, rest1], axis=1)
    sa = jnp.concatenate([-sin, z_half, rest0], axis=1)
    sb = jnp.concatenate([z_half, sin, rest0], axis=1)
    rep = LANES // hd
    return jnp.stack([jnp.tile(c, (1, rep)), jnp.tile(sa, (1, rep)), jnp.tile(sb, (1, rep))])


def _rope_apply(z, tab_ref, half):
    c, sa, sb = tab_ref[0], tab_ref[1], tab_ref[2]
    outs = []
    for s in range(z.shape[1] // LANES):
        zs = z[:, s * LANES:(s + 1) * LANES]
        outs.append(zs * c + pltpu.roll(zs, LANES - half, 1) * sa + pltpu.roll(zs, half, 1) * sb)
    return outs[0] if len(outs) == 1 else jnp.concatenate(outs, axis=1)


def _proj_kernel(x_ref, g_ref, wr_ref, wm_ref, wd_ref, wi_ref, we_ref, tr_ref, to_ref,
                 rq, rk, rv, rg, mq, mk, mv, mg, dq, dk, dv, dg, iq, ik, iw, eq, eg,
                 mk16, mv16, kbar):
    h = _mx(_rmsnorm(x_ref[...], g_ref[...]))
    ret_half = RET_DK // 2
    oth_half = MOBA_HD // ROT_DIV // 2

    z = _dot(h, wr_ref[...])
    rq[...] = _rope_apply(z[:, 0:256], tr_ref, ret_half)
    rk[...] = _rope_apply(z[:, 256:512], tr_ref, ret_half) * (RET_DK ** -0.5)
    rv[...] = z[:, 512:1024]
    rg[...] = z[:, 1024:1536]

    z = _dot(h, wm_ref[...])
    mq[...] = _rope_apply(z[:, 0:512], to_ref, oth_half)
    k_roped = _rope_apply(z[:, 512:1024], to_ref, oth_half)
    mk[...] = k_roped
    mv[...] = z[:, 1024:1536]
    mg[...] = z[:, 1536:2048]
    mk16[...] = _mx(k_roped)
    mv16[...] = _mx(z[:, 1024:1536])
    kbar[0] = jnp.mean(k_roped, axis=0, keepdims=True)

    z = _dot(h, wd_ref[...])
    dq[...] = _rope_apply(z[:, 0:512], to_ref, oth_half)
    dk[...] = _rope_apply(z[:, 512:640], to_ref, oth_half)
    dv[...] = z[:, 640:768]
    dg[...] = z[:, 768:1280]

    z = _dot(h, wi_ref[...])
    iq[...] = _rope_apply(z[:, 0:512], to_ref, oth_half)
    tail = z[:, 512:640]
    ik[...] = _rope_apply(tail, to_ref, oth_half)[:, 0:IDX_D]
    iw[...] = tail[:, IDX_D:IDX_D + IDX_H]

    z = _dot(h, we_ref[...])
    eq[...] = z[:, 0:512]
    eg[...] = z[:, 512:1024]


def _project(x2, g, w, tab_ret, tab_oth, tm):
    n = x2.shape[0]
    nt = tab_ret.shape[1] // tm
    widths = dict(rq=256, rk=256, rv=512, rg=512, mq=512, mk=512, mv=512, mg=512,
                  dq=512, dk=128, dv=128, dg=512, iq=512, ik=64, iw=8, eq=512, eg=512)
    names = list(widths)
    tok = lambda wd: pl.BlockSpec((tm, wd), lambda i: (i, 0))
    tab = pl.BlockSpec((3, tm, LANES), lambda i: (0, i % nt, 0))
    outs = pl.pallas_call(
        _proj_kernel,
        grid=(n // tm,),
        in_specs=[tok(D_MODEL), _const_spec((1, D_MODEL)),
                  _const_spec((D_MODEL, W_RET)), _const_spec((D_MODEL, W_MOBA)),
                  _const_spec((D_MODEL, W_DSA)), _const_spec((D_MODEL, W_IDX_PAD)),
                  _const_spec((D_MODEL, W_MEM)), tab, tab],
        out_specs=[tok(widths[k]) for k in names]
                  + [tok(512), tok(512), pl.BlockSpec((1, 1, 512), lambda i: (i, 0, 0))],
        out_shape=[jax.ShapeDtypeStruct((n, widths[k]), F32) for k in names]
                  + [jax.ShapeDtypeStruct((n, 512), MXU_DTYPE)] * 2
                  + [jax.ShapeDtypeStruct((n // tm, 1, 512), F32)],
        compiler_params=_params(("parallel",)),
        name="project",
    )(x2, g, w["ret"], w["moba"], w["dsa"], w["idx"], w["mem"], tab_ret, tab_oth)
    return dict(zip(names + ["mk16", "mv16", "kbar"], outs))


def _ret_kernel(q_ref, k_ref, v_ref, g_ref, s0_ref, dec_ref, qd_ref, kd_ref, cd_ref, gr_ref,
                o_ref, s_ref):
    @pl.when(pl.program_id(1) == 0)
    def _():
        s_ref[...] = s0_ref[...]

    for b in range(q_ref.shape[0]):
        q, k, v = q_ref[b], k_ref[b], v_ref[b]
        outs = []
        for h in range(RET_H):
            qh = _mx(q[:, h * RET_DK:(h + 1) * RET_DK])
            kh = k[:, h * RET_DK:(h + 1) * RET_DK]
            vh = _mx(v[:, h * RET_DV:(h + 1) * RET_DV])
            state = s_ref[b, h]
            inner = _dot_nt(qh, _mx(kh)) * dec_ref[h]
            o = _dot(_mx(inner), vh) + _dot(qh, _mx(state)) * qd_ref[h]
            s_ref[b, h] = cd_ref[h] * state + _dot_tn(_mx(kh * kd_ref[h]), vh)
            mu = jnp.mean(o, axis=-1, keepdims=True)
            d = o - mu
            var = jnp.mean(d * d, axis=-1, keepdims=True)
            outs.append(d * lax.rsqrt(var + EPS))
        o_ref[b] = jnp.concatenate(outs, axis=1) * gr_ref[...] * _silu(g_ref[b])


def _retention(pr, s0, g_ret, nb, t, chunk):
    nc = t // chunk
    lg = jnp.log1p(-jnp.exp2(-5.0 - jnp.arange(RET_H, dtype=F32)))
    n = jnp.arange(chunk, dtype=F32)
    diff = n[:, None] - n[None, :]
    dec = jnp.where(diff[None] >= 0, jnp.exp(jnp.maximum(diff, 0.0)[None] * lg[:, None, None]), 0.0)
    qd = jnp.exp((n[None, :] + 1.0) * lg[:, None])[:, :, None]
    kd = jnp.exp((chunk - 1.0 - n)[None, :] * lg[:, None])[:, :, None]
    cd = jnp.exp(chunk * lg)[:, None, None]
    sps = math.gcd(nb, RET_SEQS_PER_STEP)
    tok = lambda wd: pl.BlockSpec((sps, chunk, wd), lambda b, c: (b, c, 0))
    st = pl.BlockSpec((sps, RET_H, RET_DK, RET_DV), lambda b, c: (b, 0, 0, 0))
    seq3 = lambda a: a.reshape(nb, t, a.shape[-1])
    out, state = pl.pallas_call(
        _ret_kernel,
        grid=(nb // sps, nc),
        in_specs=[tok(256), tok(256), tok(512), tok(512), st,
                  _const_spec((RET_H, chunk, chunk)), _const_spec((RET_H, chunk, 1)),
                  _const_spec((RET_H, chunk, 1)), _const_spec((RET_H, 1, 1)), _const_spec((1, 512))],
        out_specs=[tok(512), st],
        out_shape=[jax.ShapeDtypeStruct((nb, t, 512), F32),
                   jax.ShapeDtypeStruct((nb, RET_H, RET_DK, RET_DV), F32)],
        compiler_params=_params(("parallel", "arbitrary")),
        name="retention",
    )(seq3(pr["rq"]), seq3(pr["rk"]), seq3(pr["rv"]), seq3(pr["rg"]), s0, dec, qd, kd, cd, g_ret)
    return out.reshape(nb * t, 512), state


def _topk_rows_mask(gate, valid_f, col_f, k, n_cols):
    g = jnp.where(valid_f > 0, gate, -jnp.inf)
    sel = jnp.zeros_like(gate)
    for _ in range(k):
        mx = jnp.max(g, axis=-1, keepdims=True)
        idx = jnp.min(jnp.where(g == mx, col_f, float(n_cols)), axis=-1, keepdims=True)
        pick = col_f == idx
        sel = jnp.where(pick, 1.0, sel)
        g = jnp.where(pick, -jnp.inf, g)
    return sel * valid_f


def _topk_cols_mask(gate, valid_f, row_f, k, n_rows):
    g = jnp.where(valid_f > 0, gate, -jnp.inf)
    sel = jnp.zeros_like(gate)
    for _ in range(k):
        mx = jnp.max(g, axis=0, keepdims=True)
        idx = jnp.min(jnp.where(g == mx, row_f, float(n_rows)), axis=0, keepdims=True)
        pick = row_f == idx
        sel = jnp.where(pick, 1.0, sel)
        g = jnp.where(pick, -jnp.inf, g)
    return sel * valid_f


def _sort_key(score):
    bits = lax.bitcast_convert_type(score, jnp.int32)
    key = bits ^ ((bits >> 31) & jnp.int32(0x7FFFFFFF))
    return jnp.where(score == 0.0, 0, key)


def _select_topk_bias(key_ref, bias_ref, nch, valid_fn, rows, n_cols, topk, idx_bits):
    r = rows.stop - rows.start
    col = lax.broadcasted_iota(jnp.int32, (r, n_cols), 1)
    kf = float(topk)

    def count(pred_fn):
        def cb(c, acc):
            return acc + jnp.where(pred_fn(c, key_ref[c, rows, :]), 1.0, 0.0)
        acc = lax.fori_loop(0, nch, cb, jnp.zeros((r, n_cols), F32))
        return jnp.sum(acc, axis=-1, keepdims=True)

    def bit_body(i, carry):
        thr, cnt = carry
        cand = thr + lax.shift_left(jnp.int32(1), 31 - i)
        c2 = count(lambda c, kc: kc >= cand)
        ok = c2 >= kf
        return jnp.where(ok, cand, thr), jnp.where(ok, c2, cnt)

    total = (nch * n_cols + jnp.zeros((r, 1), jnp.int32)).astype(F32)
    thr, cnt = lax.fori_loop(0, 32, bit_body, (jnp.full((r, 1), INT_MIN, jnp.int32), total))

    def tie_search():
        need = kf - count(lambda c, kc: kc > thr)

        def idx_body(i, x):
            cand = x + lax.shift_left(jnp.int32(1), idx_bits - 1 - i)
            c2 = count(lambda c, kc: (kc == thr) & (c * n_cols + col < cand))
            return jnp.where(c2 < need, cand, x)

        return lax.fori_loop(0, idx_bits, idx_body, jnp.zeros((r, 1), jnp.int32)) + 1

    jstar = lax.cond(jnp.max(cnt) > kf, tie_search, lambda: jnp.full((r, 1), INT_MAX, jnp.int32))

    def wb(c, _):
        kc = key_ref[c, rows, :]
        take = (kc > thr) | ((kc == thr) & (c * n_cols + col < jstar))
        bias_ref[c, rows, :] = jnp.where(take & valid_fn(c, rows), 0.0, NEG)
        return 0

    lax.fori_loop(0, nch, wb, 0)


def _softmax_update(s, m_ref, acc_ref, pv_fn):
    m_prev = m_ref[...]
    m_new = jnp.maximum(m_prev, jnp.max(s, axis=-1, keepdims=True))
    alpha = jnp.exp2(m_prev - m_new)
    p = jnp.concatenate([jnp.exp2(s[:, i * LANES:(i + 1) * LANES] - m_new)
                         for i in range(s.shape[1] // LANES)], axis=1)
    acc_ref[...] = alpha * acc_ref[...] + pv_fn(_mx(p))
    m_ref[...] = m_new


def _merge_states(m_ref, acc_ref):
    ms = [m_ref[i] for i in range(N_STATES)]
    m = functools.reduce(jnp.maximum, ms)
    return sum(acc_ref[i] * jnp.exp2(ms[i] - m) for i in range(N_STATES))


def _moba_p_kernel(q_ref, k_ref, v_ref, kbar_ref, g_ref, o_ref, qa_ref, m_ref, acc_ref, *, nblk):
    qi = pl.program_id(2)
    tq = MOBA_BLOCK
    nhp = q_ref.shape[1] // LANES
    lane = lax.broadcasted_iota(jnp.int32, (tq, LANES), 1)
    lo = lane < HALF
    blk_r = lax.broadcasted_iota(jnp.int32, (nblk, 2 * tq), 0)
    valid_f = jnp.where(blk_r < qi, 1.0, 0.0)
    causal = (lax.broadcasted_iota(jnp.int32, (tq, tq), 1)
              <= lax.broadcasted_iota(jnp.int32, (tq, tq), 0))
    one = jnp.ones((), MXU_DTYPE)
    cq = MOBA_HD ** -0.5 * LOG2E

    def block_update(hp, par, k0, n, mask):
        sl = slice(hp * LANES, (hp + 1) * LANES)
        kb = k_ref[pl.ds(k0, tq), sl]
        vb = v_ref[pl.ds(k0, tq), sl]
        s, v = [], []
        for head in range(2):
            rows = slice(head * tq, (head + 1) * tq)
            ind = _mx(jnp.where(lane == (HALF + n if head == 0 else n), 1.0, 0.0))
            k_h = jnp.where(lo, kb, ind) if head == 0 else jnp.where(lo, ind, kb)
            v.append(jnp.where(lo, vb, one) if head == 0 else jnp.where(lo, one, vb))
            s.append(_dot_nt(qa_ref[hp, rows, :], k_h))
            if mask is None:
                _softmax_update(s[head], m_ref.at[hp, par, rows], acc_ref.at[hp, par, rows],
                                lambda p, v_h=v[head]: _dot(p, v_h))
        if mask is not None:
            both = jnp.concatenate([jnp.where(mask, x, NEG) for x in s], axis=0)
            _softmax_update(both, m_ref.at[hp, par], acc_ref.at[hp, par],
                            lambda p: jnp.concatenate([_dot(p[:tq], v[0]), _dot(p[tq:], v[1])], axis=0))

    own0 = pl.multiple_of(qi * tq, tq)
    m_ref[...] = jnp.full(m_ref.shape, NEG, F32)
    acc_ref[...] = jnp.zeros(acc_ref.shape, F32)
    kbar = kbar_ref[...].reshape(nblk, nhp * LANES)
    pad = jnp.zeros((tq, HALF - nblk), F32)
    for hp in range(nhp):
        sl = slice(hp * LANES, (hp + 1) * LANES)
        q2 = q_ref[:, sl]
        qst = jnp.concatenate([jnp.where(lo, q2, 0.0), jnp.where(lo, 0.0, q2)], axis=0)
        gate_t = _dot_nt(_mx(kbar[:, sl]), _mx(qst))
        sel_t = _topk_cols_mask(gate_t, valid_f, blk_r.astype(F32), MOBA_TOPK, nblk)
        bias = jnp.where(sel_t > 0.0, 0.0, NEG).T
        qa = jnp.concatenate(
            [jnp.concatenate([q2[:, :HALF] * cq, bias[:tq], pad], axis=1),
             jnp.concatenate([bias[tq:], pad, q2[:, HALF:] * cq], axis=1)], axis=0)
        qa_ref[hp] = _mx(qa)
        block_update(hp, 0, own0, -LANES, causal)

    def body(i, _):
        for par in range(N_STATES):
            n = N_STATES * i + par
            k0 = pl.multiple_of(jnp.minimum(n, nblk - 1) * tq, tq)
            for hp in range(nhp):
                block_update(hp, par, k0, n, None)
        return 0

    lax.fori_loop(0, (qi + N_STATES - 1) // N_STATES, body, 0)
    for hp in range(nhp):
        sl = slice(hp * LANES, (hp + 1) * LANES)
        acc = _merge_states(m_ref.at[hp], acc_ref.at[hp])
        a0 = acc[0:tq, :]
        a1 = acc[tq:2 * tq, :]
        o = jnp.where(lo, a0 / _swap_halves(a0), a1 / _swap_halves(a1))
        o_ref[:, sl] = o * _silu(g_ref[:, sl])


def _moba_prompt(pr, nb, t):
    nblk = t // MOBA_BLOCK
    tq = MOBA_BLOCK
    wd = MOBA_H * MOBA_HD
    nhp = MOBA_PAIRS_PER_STEP
    ws = nhp * LANES
    qspec = pl.BlockSpec((tq, ws), lambda b, hg, qi: (b * nblk + qi, hg))
    kspec = pl.BlockSpec((t, ws), lambda b, hg, qi: (b, hg))
    return pl.pallas_call(
        functools.partial(_moba_p_kernel, nblk=nblk),
        grid=(nb, wd // ws, nblk),
        in_specs=[qspec, kspec, kspec, pl.BlockSpec((nblk, 1, ws), lambda b, hg, qi: (b, 0, hg)), qspec],
        out_specs=qspec,
        out_shape=jax.ShapeDtypeStruct((nb * t, wd), F32),
        scratch_shapes=[pltpu.VMEM((nhp, 2 * tq, LANES), MXU_DTYPE),
                        pltpu.VMEM((nhp, N_STATES, 2 * tq, LANES), F32),
                        pltpu.VMEM((nhp, N_STATES, 2 * tq, LANES), F32)],
        compiler_params=_params(("parallel", "parallel", "arbitrary")),
        name="moba_prompt",
    )(pr["mq"], pr["mk16"], pr["mv16"], pr["kbar"], pr["mg"])


def _select_topk_bias_t(key_ref, bias_ref, nch, valid_fn, n, topk, idx_bits):
    rowk = lax.broadcasted_iota(jnp.int32, (n, n), 0)
    kf = float(topk)
    sub = 8

    def count(pred_fn):
        def cb(i, acc):
            for par in range(2):
                c = 2 * i + par
                hit = jnp.where(pred_fn(c, key_ref[c]), 1.0, 0.0)
                acc = acc + jnp.sum(hit.reshape(n // sub, sub, n), axis=0)
            return acc
        acc = lax.fori_loop(0, (nch + 1) // 2, cb, jnp.zeros((sub, n), F32))
        return jnp.sum(acc, axis=0, keepdims=True)

    def bit_body(i, carry):
        thr, cnt = carry
        cand = thr + lax.shift_left(jnp.int32(1), 31 - i)
        c2 = count(lambda c, kc: kc >= cand)
        ok = c2 >= kf
        return jnp.where(ok, cand, thr), jnp.where(ok, c2, cnt)

    total = (nch * n + jnp.zeros((1, n), jnp.int32)).astype(F32)
    thr, cnt = lax.fori_loop(0, 32, bit_body, (jnp.full((1, n), INT_MIN, jnp.int32), total))

    def tie_search():
        need = kf - count(lambda c, kc: kc > thr)

        def idx_body(i, x):
            cand = x + lax.shift_left(jnp.int32(1), idx_bits - 1 - i)
            c2 = count(lambda c, kc: (kc == thr) & (c * n + rowk < cand))
            return jnp.where(c2 < need, cand, x)

        return lax.fori_loop(0, idx_bits, idx_body, jnp.zeros((1, n), jnp.int32)) + 1

    jstar = lax.cond(jnp.max(cnt) > kf, tie_search, lambda: jnp.full((1, n), INT_MAX, jnp.int32))

    def wb(c, _):
        kc = key_ref[c]
        take = (kc > thr) | ((kc == thr) & (c * n + rowk < jstar))
        bias_ref[c] = jnp.where(take & valid_fn(c), 0.0, NEG).T
        return 0

    lax.fori_loop(0, nch, wb, 0)


def _dsa_p_kernel(q_ref, iq_ref, iwt_ref, k_ref, v_ref, ki_ref, g_ref, o_ref,
                  key_ref, bias_ref, qs_ref, m_ref, acc_ref, *, topk, idx_bits):
    qi = pl.program_id(1)
    tq = q_ref.shape[0]
    nch = qi + 1
    grp = DSA_H // DSA_KVH
    lane = lax.broadcasted_iota(jnp.int32, (tq, LANES), 1)
    lo = lane < HALF

    def valid_t(c):
        rowk = lax.broadcasted_iota(jnp.int32, (tq, tq), 0)
        colq = lax.broadcasted_iota(jnp.int32, (tq, tq), 1)
        return (rowk <= colq) | (c < qi)

    iq = iq_ref[...]
    wrow = (iwt_ref[...] * (IDX_H ** -0.5)) * (IDX_D ** -0.5)
    iq_all = jnp.concatenate([_mx(iq[:, h * IDX_D:(h + 1) * IDX_D]) for h in range(IDX_H)], axis=0)

    def score_body(i, _):
        for par in range(2):
            c = 2 * i + par
            cc = jnp.minimum(c, qi)
            kic = _mx(ki_ref[pl.ds(pl.multiple_of(cc * tq, tq), tq), :])
            rel = _dot_nt(kic, iq_all)
            sc = jnp.zeros((tq, tq), F32)
            for h in range(IDX_H):
                sc = sc + jnp.maximum(rel[:, h * tq:(h + 1) * tq], 0.0) * wrow[h:h + 1, :]
            key = _sort_key(jnp.where(valid_t(c), sc, -jnp.inf))
            key_ref[c] = jnp.where(c < nch, key, INT_MIN)
        return 0

    lax.fori_loop(0, (nch + 1) // 2, score_body, 0)
    _select_topk_bias_t(key_ref, bias_ref, nch, valid_t, tq, topk, idx_bits)

    q = q_ref[...]
    cq = DSA_HD ** -0.5 * LOG2E
    for g in range(DSA_H):
        kvh, j = g // grp, g % grp
        slab = q[:, (g // 2) * LANES:(g // 2 + 1) * LANES]
        if g % 2 != kvh:
            slab = _swap_halves(slab)
        in_kv = lo if kvh == 0 else jnp.logical_not(lo)
        qs_ref[kvh, j * tq:(j + 1) * tq, :] = _mx(jnp.where(in_kv, slab, 0.0) * cq)
    m_ref[...] = jnp.full(m_ref.shape, NEG, F32)
    acc_ref[...] = jnp.zeros(acc_ref.shape, F32)

    def att_body(i, _):
        for par in range(N_STATES):
            c = N_STATES * i + par
            cc = jnp.minimum(c, qi)
            k0 = pl.multiple_of(cc * tq, tq)
            kb = _mx(k_ref[pl.ds(k0, tq), :])
            vb = v_ref[pl.ds(k0, tq), :]
            bias = bias_ref[cc] + jnp.where(c < nch, 0.0, NEG)
            bias4 = jnp.concatenate([bias] * grp, axis=0)
            for kvh in range(DSA_KVH):
                va = _mx(jnp.where(lo, vb, 1.0) if kvh == 0 else jnp.where(lo, 1.0, vb))
                s = _dot_nt(qs_ref[kvh], kb) + bias4
                _softmax_update(s, m_ref.at[par, kvh], acc_ref.at[par, kvh],
                                lambda p, va=va: _dot(p, va))
        return 0

    lax.fori_loop(0, (nch + N_STATES - 1) // N_STATES, att_body, 0)

    g_all = g_ref[...]
    acc = [_merge_states(m_ref.at[:, kvh], acc_ref.at[:, kvh]) for kvh in range(DSA_KVH)]
    for pair in range(DSA_H // 2):
        res = []
        for jj in range(2):
            g = 2 * pair + jj
            kvh, j = g // grp, g % grp
            a = acc[kvh][j * tq:(j + 1) * tq, :]
            o = a / _swap_halves(a)
            res.append(o if jj == kvh else _swap_halves(o))
        sl = slice(pair * LANES, (pair + 1) * LANES)
        o_ref[:, sl] = jnp.where(lo, res[0], res[1]) * _silu(g_all[:, sl])


def _dsa_prompt(pr, nb, t):
    tq = 256
    nq = t // tq
    topk = min(DSA_TOPK, t // 4)
    idx_bits = max(1, math.ceil(math.log2(t)))
    grp = DSA_H // DSA_KVH
    tok = lambda wd: pl.BlockSpec((tq, wd), lambda b, qi: (b * nq + qi, 0))
    seq = lambda wd: pl.BlockSpec((t, wd), lambda b, qi: (b, 0))
    iwt = pr["iw"].T
    return pl.pallas_call(
        functools.partial(_dsa_p_kernel, topk=topk, idx_bits=idx_bits),
        grid=(nb, nq),
        in_specs=[tok(512), tok(512), pl.BlockSpec((IDX_H, tq), lambda b, qi: (0, b * nq + qi)),
                  seq(LANES), seq(LANES), seq(IDX_D), tok(512)],
        out_specs=tok(512),
        out_shape=jax.ShapeDtypeStruct((nb * t, DSA_H * DSA_HD), F32),
        scratch_shapes=[pltpu.VMEM((nq, tq, tq), jnp.int32), pltpu.VMEM((nq, tq, tq), F32),
                        pltpu.VMEM((DSA_KVH, grp * tq, LANES), MXU_DTYPE),
                        pltpu.VMEM((N_STATES, DSA_KVH, grp * tq, LANES), F32),
                        pltpu.VMEM((N_STATES, DSA_KVH, grp * tq, LANES), F32)],
        compiler_params=_params(("parallel", "arbitrary")),
        name="dsa_prompt",
    )(pr["dq"], pr["iq"], iwt, pr["dk"], pr["dv"], pr["ik"], pr["dg"])


def _memkv_kernel(x_ref, g_ref, w_ref, k_ref, v_ref):
    h = _mx(_rmsnorm(x_ref[...], g_ref[...]))
    z = _dot(h, w_ref[...])
    half = MEM_H * MEM_HD
    k_ref[...] = z[:, :half]
    v_ref[...] = z[:, half:]


def _mem_kv(mem2, g, w):
    n = mem2.shape[0]
    tm = 256
    half = MEM_H * MEM_HD
    tok = lambda wd: pl.BlockSpec((tm, wd), lambda i: (i, 0))
    return pl.pallas_call(
        _memkv_kernel,
        grid=(n // tm,),
        in_specs=[tok(D_MODEL), _const_spec((1, D_MODEL)), _const_spec((D_MODEL, 2 * half))],
        out_specs=[tok(half), tok(half)],
        out_shape=[jax.ShapeDtypeStruct((n, half), F32)] * 2,
        compiler_params=_params(("parallel",)),
        name="mem_kv",
    )(mem2, g, w)


def _mem_kernel(q_ref, k_ref, v_ref, g_ref, o_ref):
    q, k, v = q_ref[...], k_ref[...], v_ref[...]
    outs = []
    for h in range(MEM_H):
        sl = slice(h * MEM_HD, (h + 1) * MEM_HD)
        s = _dot_nt(_mx(q[:, sl]), _mx(k[:, sl])) * (MEM_HD ** -0.5)
        m = jnp.max(s, axis=-1, keepdims=True)
        p = jnp.exp(s - m)
        p = p / jnp.sum(p, axis=-1, keepdims=True)
        outs.append(_dot(_mx(p), _mx(v[:, sl])))
    o_ref[...] = jnp.concatenate(outs, axis=1) * _silu(g_ref[...])


def _mem_attend(q, mk, mv, g, nb, t, tq):
    nq = t // tq
    ntok = mk.shape[0] // nb
    tok = pl.BlockSpec((tq, 512), lambda b, i: (b * nq + i, 0))
    kv = pl.BlockSpec((ntok, 512), lambda b, i: (b, 0))
    return pl.pallas_call(
        _mem_kernel,
        grid=(nb, nq),
        in_specs=[tok, kv, kv, tok],
        out_specs=tok,
        out_shape=jax.ShapeDtypeStruct((nb * t, 512), F32),
        compiler_params=_params(("parallel", "arbitrary")),
        name="mem_attend",
    )(q, mk, mv, g)


def _merge_kernel(x_ref, g_ref, wbg_ref, o0, o1, o2, o3, wb_ref, wo_ref, gf_ref, y_ref, *, final):
    x = x_ref[...]
    h = _mx(_rmsnorm(x, g_ref[...]))
    merged = jnp.zeros(x.shape, F32)
    for b, o in enumerate((o0, o1, o2, o3)):
        gate = _sigmoid(_dot(h, wbg_ref[:, b * D_MODEL:(b + 1) * D_MODEL]))
        merged = merged + gate * _dot(_mx(o[...]), wb_ref[b])
    y = x + _dot(_mx(merged), wo_ref[...])
    y_ref[...] = _rmsnorm(y, gf_ref[...]) if final else y


def _merge(x2, g, wbg, outs, wb, wo, gf, final, tm):
    n = x2.shape[0]
    tok = lambda wd: pl.BlockSpec((tm, wd), lambda i: (i, 0))
    return pl.pallas_call(
        functools.partial(_merge_kernel, final=final),
        grid=(n // tm,),
        in_specs=[tok(D_MODEL), _const_spec((1, D_MODEL)), _const_spec((D_MODEL, W_BG)),
                  tok(BR_W), tok(BR_W), tok(BR_W), tok(BR_W),
                  _const_spec((N_BRANCH, BR_W, D_MODEL)), _const_spec((D_MODEL, D_MODEL)),
                  _const_spec((1, D_MODEL))],
        out_specs=tok(D_MODEL),
        out_shape=jax.ShapeDtypeStruct((n, D_MODEL), F32),
        compiler_params=_params(("parallel",)),
        name="merge",
    )(x2, g, wbg, *outs, wb, wo, gf)


def _pages_t(cache):
    nd = cache.ndim
    perm = (0, 1) + tuple(range(3, nd)) + (2,)
    t = jnp.transpose(cache, perm)
    return t.reshape(t.shape[0], t.shape[1], -1, PAGE)


def _page_copy(cache_ref, buf_ref, sem_ref, slot, layer, page, i):
    return pltpu.make_async_copy(cache_ref.at[layer, page],
                                 buf_ref.at[slot, :, i * PAGE:(i + 1) * PAGE], sem_ref.at[slot])


def _start_pages(cache_ref, buf_ref, sem_ref, slot, layer, pt_ref, b, j, pps):
    for i in range(pps):
        _page_copy(cache_ref, buf_ref, sem_ref, slot, layer, pt_ref[b, j * pps + i], i).start(priority=i % 2)


def _wait_pages(cache_ref, buf_ref, sem_ref, slot, layer, pps):
    for i in range(pps):
        _page_copy(cache_ref, buf_ref, sem_ref, slot, layer, 0, i).wait()


def _sample_step(nj):
    b, ph, j = pl.program_id(0), pl.program_id(1), pl.program_id(2)
    return (b * 2 + ph) * nj + j, pl.num_programs(0) * 2 * nj


def _prefetch_pages(t, total, issue):
    @pl.when(t == 0)
    def _():
        issue(t)

    @pl.when(t + 1 < total)
    def _():
        issue(t + 1)


def _pad_rows(x, rows):
    return jnp.concatenate([x, jnp.zeros((rows - x.shape[0], x.shape[1]), x.dtype)], axis=0)


def _moba_s_kernel(pt_ref, qbd_ref, kn_ref, vn_ref, g_ref, ck_ref, cv_ref, o_ref,
                   s_ref, rl_ref, acc_ref, buf_ref, sem_ref, *, pps, nj, layer):
    ph, j = pl.program_id(1), pl.program_id(2)
    nq = kn_ref.shape[0]
    nr = MOBA_H * nq
    wdt = pps * PAGE
    bps = wdt // MOBA_BLOCK
    nblk = nj * bps
    c2 = MOBA_HD ** -0.5 * LOG2E
    t, total = _sample_step(nj)
    slot = t % 2

    def issue(tt):
        bb, pp, jj = tt // (2 * nj), (tt // nj) % 2, tt % nj
        for phase, cache_ref in enumerate((ck_ref, cv_ref)):
            @pl.when(pp == phase)
            def _(cache_ref=cache_ref):
                _start_pages(cache_ref, buf_ref, sem_ref, tt % 2, layer, pt_ref, bb, jj, pps)

    _prefetch_pages(t, total, issue)
    _wait_pages(ck_ref, buf_ref, sem_ref, slot, layer, pps)

    @pl.when(ph == 0)
    def _():
        s_ref[j] = _dot(_mx(qbd_ref[0]), _mx(buf_ref[slot]))

    @pl.when((ph == 1) & (j == 0))
    def _():
        blane = lax.broadcasted_iota(jnp.int32, (nr, nblk), 1)
        gate = jnp.zeros((nr, nblk), F32)
        bmax = jnp.full((nr, nblk), NEG, F32)
        for jj in range(nj):
            x = s_ref[jj]
            for b in range(bps):
                xb = x[:, b * MOBA_BLOCK:(b + 1) * MOBA_BLOCK]
                here = blane == jj * bps + b
                gate = jnp.where(here, jnp.sum(xb, axis=-1, keepdims=True) * (1.0 / MOBA_BLOCK), gate)
                bmax = jnp.where(here, jnp.max(xb, axis=-1, keepdims=True), bmax)
        sel = _topk_rows_mask(gate, jnp.ones((nr, nblk), F32), blane.astype(F32),
                              min(MOBA_TOPK, nblk), nblk)

        lane = lax.broadcasted_iota(jnp.int32, (nr, LANES), 1)
        qrow = lax.broadcasted_iota(jnp.int32, (nr, LANES), 0) % nq
        s_own = jnp.where(lane <= qrow, _dot_nt(_mx(qbd_ref[0]), _mx(_pad_rows(kn_ref[...], LANES))), NEG)
        m = jnp.maximum(jnp.max(jnp.where(sel > 0, bmax, NEG), axis=-1, keepdims=True),
                        jnp.max(s_own, axis=-1, keepdims=True))
        p_own = jnp.exp2((s_own - m) * c2)
        l = jnp.sum(p_own, axis=-1, keepdims=True)
        for jj in range(nj):
            x = s_ref[jj]
            parts = []
            for b in range(bps):
                on = jnp.sum(jnp.where(blane == jj * bps + b, sel, 0.0), axis=-1, keepdims=True) > 0.0
                xb = x[:, b * MOBA_BLOCK:(b + 1) * MOBA_BLOCK]
                parts.append(jnp.where(on, jnp.exp2((xb - m) * c2), 0.0))
            p = jnp.concatenate(parts, axis=1)
            l = l + jnp.sum(p, axis=-1, keepdims=True)
            s_ref[jj] = p
        rl = 1.0 / l
        rl_ref[...] = jnp.broadcast_to(rl, rl_ref.shape)
        acc_ref[...] = _dot(_mx(p_own * rl), _mx(_pad_rows(vn_ref[...], LANES)))

    @pl.when(ph == 1)
    def _():
        p = _mx(s_ref[j] * rl_ref[:, 0:1])
        acc_ref[...] += _dot_nt(p, _mx(buf_ref[slot]))

    @pl.when((ph == 1) & (j == nj - 1))
    def _():
        acc = acc_ref[...]
        parts = [acc[h * nq:(h + 1) * nq, h * MOBA_HD:(h + 1) * MOBA_HD] for h in range(MOBA_H)]
        o_ref[...] = jnp.concatenate(parts, axis=1) * _silu(g_ref[...])


def _moba_sample(ps, ck_t, cv_t, layer, page_table, nb, nq):
    n_pages = page_table.shape[1]
    pps = min(SAMPLE_PAGES_PER_STEP, n_pages)
    nj = n_pages // pps
    wd = MOBA_H * MOBA_HD
    nr = MOBA_H * nq
    q4 = ps["mq"].reshape(nb, nq, MOBA_H, MOBA_HD).transpose(0, 2, 1, 3)
    eye = jnp.eye(MOBA_H, dtype=F32)
    qbd = (q4[:, :, :, None, :] * eye[None, :, None, :, None]).reshape(nb, nr, wd)
    tok = pl.BlockSpec((nq, wd), lambda b, ph, j, pt: (b, 0))
    hbm = pl.BlockSpec(memory_space=pl.ANY)
    grid_spec = pltpu.PrefetchScalarGridSpec(
        num_scalar_prefetch=1,
        grid=(nb, 2, nj),
        in_specs=[pl.BlockSpec((1, nr, wd), lambda b, ph, j, pt: (b, 0, 0)), tok, tok, tok, hbm, hbm],
        out_specs=tok,
        scratch_shapes=[pltpu.VMEM((nj, nr, pps * PAGE), F32),
                        pltpu.VMEM((nr, LANES), F32),
                        pltpu.VMEM((nr, wd), F32),
                        pltpu.VMEM((2, wd, pps * PAGE), F32),
                        pltpu.SemaphoreType.DMA((2,))],
    )
    return pl.pallas_call(
        functools.partial(_moba_s_kernel, pps=pps, nj=nj, layer=layer),
        grid_spec=grid_spec,
        out_shape=jax.ShapeDtypeStruct((nb * nq, wd), F32),
        compiler_params=_params(("arbitrary", "arbitrary", "arbitrary")),
        name="moba_sample",
    )(page_table, qbd, ps["mk"], ps["mv"], ps["mg"], ck_t, cv_t)


def _dsa_s_kernel(pt_ref, qi_ref, wi_ref, qz_ref, ikn_ref, kn_ref, vn_ref, g_ref,
                  ci_ref, ck_ref, cv_ref, o_ref, key_ref, bias_ref, m_ref, l_ref, acc_ref,
                  ibuf_ref, kbuf_ref, vbuf_ref, isem_ref, ksem_ref, vsem_ref,
                  *, pps, nj, topk, idx_bits, layer):
    ph, j = pl.program_id(1), pl.program_id(2)
    nq = ikn_ref.shape[0]
    wdt = pps * PAGE
    scale = DSA_HD ** -0.5
    lane = lax.broadcasted_iota(jnp.int32, (nq, wdt), 1)
    qrow = lax.broadcasted_iota(jnp.int32, (nq, wdt), 0)
    new_valid = lane <= qrow
    t, total = _sample_step(nj)
    slot = t % 2

    def issue(tt):
        bb, pp, jj = tt // (2 * nj), (tt // nj) % 2, tt % nj

        @pl.when(pp == 0)
        def _():
            _start_pages(ci_ref, ibuf_ref, isem_ref, tt % 2, layer, pt_ref, bb, jj, pps)

        @pl.when(pp == 1)
        def _():
            _start_pages(ck_ref, kbuf_ref, ksem_ref, tt % 2, layer, pt_ref, bb, jj, pps)
            _start_pages(cv_ref, vbuf_ref, vsem_ref, tt % 2, layer, pt_ref, bb, jj, pps)

    _prefetch_pages(t, total, issue)

    def idx_scores(rel):
        rel = jnp.maximum(rel * (IDX_D ** -0.5), 0.0) * wi_ref[0]
        return jnp.sum(rel.reshape(IDX_H, nq, rel.shape[1]), axis=0)

    @pl.when(ph == 0)
    def _():
        _wait_pages(ci_ref, ibuf_ref, isem_ref, slot, layer, pps)
        key_ref[j] = _sort_key(idx_scores(_dot(_mx(qi_ref[0]), _mx(ibuf_ref[slot]))))

    def attend(s, pv_fn):
        m = m_ref[...]
        m_new = jnp.maximum(m, jnp.max(s, axis=-1, keepdims=True))
        alpha = jnp.exp(m - m_new)
        p = jnp.exp(s - m_new)
        l_ref[...] = alpha * l_ref[...] + jnp.sum(p, axis=-1, keepdims=True)
        acc_ref[...] = alpha * acc_ref[...] + pv_fn(_mx(p))
        m_ref[...] = m_new

    @pl.when((ph == 1) & (j == 0))
    def _():
        ik_new = _mx(_pad_rows(ikn_ref[...], wdt))
        s_new = idx_scores(_dot_nt(_mx(qi_ref[0]), ik_new))
        key_ref[nj] = _sort_key(jnp.where(new_valid, s_new, -jnp.inf))

        def valid(c, rows):
            return new_valid | (c < nj)
        _select_topk_bias(key_ref, bias_ref, nj + 1, valid, slice(0, nq), wdt, topk, idx_bits)

        m_ref[...] = jnp.full(m_ref.shape, NEG, F32)
        l_ref[...] = jnp.zeros(l_ref.shape, F32)
        acc_ref[...] = jnp.zeros(acc_ref.shape, F32)
        k_new = _mx(_pad_rows(kn_ref[...], wdt))
        v_new = _mx(_pad_rows(vn_ref[...], wdt))
        s = _dot_nt(_mx(qz_ref[0]), k_new) * scale + jnp.concatenate([bias_ref[nj]] * DSA_H, axis=0)
        attend(s, lambda p: _dot(p, v_new))

    @pl.when(ph == 1)
    def _():
        _wait_pages(ck_ref, kbuf_ref, ksem_ref, slot, layer, pps)
        _wait_pages(cv_ref, vbuf_ref, vsem_ref, slot, layer, pps)
        s = (_dot(_mx(qz_ref[0]), _mx(kbuf_ref[slot])) * scale
             + jnp.concatenate([bias_ref[j]] * DSA_H, axis=0))
        attend(s, lambda p: _dot_nt(p, _mx(vbuf_ref[slot])))

    @pl.when((ph == 1) & (j == nj - 1))
    def _():
        o = acc_ref[...] / l_ref[...]
        grp = DSA_H // DSA_KVH
        parts = []
        for g in range(DSA_H):
            kvh = g // grp
            parts.append(o[g * nq:(g + 1) * nq, kvh * DSA_HD:(kvh + 1) * DSA_HD])
        o_ref[...] = jnp.concatenate(parts, axis=1) * _silu(g_ref[...])


def _dsa_sample(ps, ck_t, cv_t, ci_t, layer, page_table, nb, nq):
    n_pages = page_table.shape[1]
    pps = min(SAMPLE_PAGES_PER_STEP, n_pages)
    nj = n_pages // pps
    n_keys = n_pages * PAGE + nq
    topk = min(DSA_TOPK, n_keys // 4)
    idx_bits = max(1, math.ceil(math.log2((nj + 1) * pps * PAGE)))
    grp = DSA_H // DSA_KVH
    qi = ps["iq"].reshape(nb, nq, IDX_H, IDX_D).transpose(0, 2, 1, 3).reshape(nb, IDX_H * nq, IDX_D)
    wi = (ps["iw"].reshape(nb, nq, IDX_H).transpose(0, 2, 1) * (IDX_H ** -0.5)).reshape(nb, IDX_H * nq, 1)
    q4 = ps["dq"].reshape(nb, nq, DSA_H, DSA_HD).transpose(0, 2, 1, 3)
    kv_of = (jnp.arange(DSA_H) // grp)[:, None] == jnp.arange(DSA_KVH)[None, :]
    qz = (q4[:, :, :, None, :] * kv_of.astype(F32)[None, :, None, :, None]).reshape(
        nb, DSA_H * nq, DSA_KVH * DSA_HD)
    per_b = lambda r, c: pl.BlockSpec((1, r, c), lambda b, ph, j, pt: (b, 0, 0))
    tok = lambda wd: pl.BlockSpec((nq, wd), lambda b, ph, j, pt: (b, 0))
    hbm = pl.BlockSpec(memory_space=pl.ANY)
    grid_spec = pltpu.PrefetchScalarGridSpec(
        num_scalar_prefetch=1,
        grid=(nb, 2, nj),
        in_specs=[per_b(IDX_H * nq, IDX_D), per_b(IDX_H * nq, 1), per_b(DSA_H * nq, LANES),
                  tok(IDX_D), tok(LANES), tok(LANES), tok(512), hbm, hbm, hbm],
        out_specs=tok(512),
        scratch_shapes=[pltpu.VMEM((nj + 1, nq, pps * PAGE), jnp.int32),
                        pltpu.VMEM((nj + 1, nq, pps * PAGE), F32),
                        pltpu.VMEM((DSA_H * nq, 1), F32),
                        pltpu.VMEM((DSA_H * nq, 1), F32),
                        pltpu.VMEM((DSA_H * nq, LANES), F32),
                        pltpu.VMEM((2, IDX_D, pps * PAGE), F32),
                        pltpu.VMEM((2, LANES, pps * PAGE), F32),
                        pltpu.VMEM((2, LANES, pps * PAGE), F32),
                        pltpu.SemaphoreType.DMA((2,)),
                        pltpu.SemaphoreType.DMA((2,)),
                        pltpu.SemaphoreType.DMA((2,))],
    )
    return pl.pallas_call(
        functools.partial(_dsa_s_kernel, pps=pps, nj=nj, topk=topk, idx_bits=idx_bits, layer=layer),
        grid_spec=grid_spec,
        out_shape=jax.ShapeDtypeStruct((nb * nq, 512), F32),
        compiler_params=_params(("arbitrary", "arbitrary", "arbitrary")),
        name="dsa_sample",
    )(page_table, qi, wi, qz, ps["ik"], ps["dk"], ps["dv"], ps["dg"], ci_t, ck_t, cv_t)


def _layer_weights(w_in_l):
    cuts = np.cumsum([0, W_RET, W_MOBA, W_DSA, W_IDX, W_MEM, W_BG])
    seg = [w_in_l[:, int(a):int(b)] for a, b in zip(cuts[:-1], cuts[1:])]
    idx = jnp.pad(seg[3], ((0, 0), (0, W_IDX_PAD - W_IDX)))
    return dict(ret=_mx(seg[0]), moba=_mx(seg[1]), dsa=_mx(seg[2]), idx=_mx(idx),
                mem=_mx(seg[4]), bg=_mx(seg[5]))


def kernel(x_prompt, x_sample, cache_moba_k, cache_moba_v, cache_dsa_k, cache_dsa_v, cache_dsa_idx_k,
           cache_mem_k, cache_mem_v, state_ret, page_table, mem_prompt, g_norm, w_in, g_ret, g_mem,
           w_mem_kv, w_branch, w_out, g_final):
    nb_p, t_p, d = x_prompt.shape
    nb_s, t_s, _ = x_sample.shape
    depth = w_in.shape[0]
    past = page_table.shape[1] * PAGE
    pos_p = jnp.arange(t_p, dtype=jnp.int32)
    pos_s = past + jnp.arange(t_s, dtype=jnp.int32)
    tm_p = 256
    n_s = nb_s * t_s
    tab_p = (_rope_tables(pos_p, RET_DK, RET_THETA, RET_DK),
             _rope_tables(pos_p, MOBA_HD // ROT_DIV, ROPE_THETA, MOBA_HD))
    tab_s = tuple(jnp.tile(_rope_tables(pos_s, r, th, 64), (1, nb_s, 1))
                  for r, th in ((RET_DK, RET_THETA), (MOBA_HD // ROT_DIV, ROPE_THETA)))
    s0_p = jnp.zeros((nb_p, RET_H, RET_DK, RET_DV), F32)
    mem2 = mem_prompt.reshape(nb_p * mem_prompt.shape[1], d)
    gf = g_final.reshape(1, d)
    moba_kt, moba_vt = _pages_t(cache_moba_k), _pages_t(cache_moba_v)
    dsa_kt, dsa_vt, dsa_it = _pages_t(cache_dsa_k), _pages_t(cache_dsa_v), _pages_t(cache_dsa_idx_k)

    xp = x_prompt.reshape(nb_p * t_p, d)
    xs = x_sample.reshape(n_s, d)
    acc = {k: [] for k in ("pmk", "pmv", "pdk", "pdv", "pdi", "pret", "pek", "pev",
                           "smk", "smv", "sdk", "sdv", "sdi", "sret")}
    for l in range(depth):
        w = _layer_weights(w_in[l])
        gn = g_norm[l].reshape(1, d)
        gr = g_ret[l].reshape(1, RET_H * RET_DV)
        wb = _mx(w_branch[l])
        wo = _mx(w_out[l])
        final = l == depth - 1

        pr = _project(xp, gn, w, tab_p[0], tab_p[1], tm_p)
        o_ret, st_p = _retention(pr, s0_p, gr, nb_p, t_p, RET_CHUNK)
        o_moba = _moba_prompt(pr, nb_p, t_p)
        o_dsa = _dsa_prompt(pr, nb_p, t_p)
        mk, mv = _mem_kv(mem2, g_mem[l].reshape(1, d), _mx(w_mem_kv[l]))
        o_mem = _mem_attend(pr["eq"], mk, mv, pr["eg"], nb_p, t_p, 256)
        xp = _merge(xp, gn, w["bg"], (o_ret, o_moba, o_dsa, o_mem), wb, wo, gf, final, tm_p)
        acc["pmk"].append(pr["mk"]); acc["pmv"].append(pr["mv"])
        acc["pdk"].append(pr["dk"]); acc["pdv"].append(pr["dv"]); acc["pdi"].append(pr["ik"])
        acc["pret"].append(st_p); acc["pek"].append(mk); acc["pev"].append(mv)

        ps = _project(xs, gn, w, tab_s[0], tab_s[1], n_s)
        o_ret_s, st_s = _retention(ps, state_ret[l], gr, nb_s, t_s, t_s)
        o_moba_s = _moba_sample(ps, moba_kt, moba_vt, l, page_table, nb_s, t_s)
        o_dsa_s = _dsa_sample(ps, dsa_kt, dsa_vt, dsa_it, l, page_table, nb_s, t_s)
        mk_s = cache_mem_k[l].reshape(nb_s * cache_mem_k.shape[2], MEM_H * MEM_HD)
        mv_s = cache_mem_v[l].reshape(nb_s * cache_mem_v.shape[2], MEM_H * MEM_HD)
        o_mem_s = _mem_attend(ps["eq"], mk_s, mv_s, ps["eg"], nb_s, t_s, t_s)
        xs = _merge(xs, gn, w["bg"], (o_ret_s, o_moba_s, o_dsa_s, o_mem_s), wb, wo, gf, final, n_s)
        acc["smk"].append(ps["mk"]); acc["smv"].append(ps["mv"])
        acc["sdk"].append(ps["dk"]); acc["sdv"].append(ps["dv"]); acc["sdi"].append(ps["ik"])
        acc["sret"].append(st_s)

    st = lambda k, shape: jnp.stack(acc[k]).reshape((depth,) + shape)
    ntok = mem_prompt.shape[1]
    return (xp.reshape(nb_p, t_p, d), xs.reshape(nb_s, t_s, d),
            st("pmk", (nb_p, t_p, MOBA_H, MOBA_HD)), st("pmv", (nb_p, t_p, MOBA_H, MOBA_HD)),
            st("pdk", (nb_p, t_p, DSA_KVH, DSA_HD)), st("pdv", (nb_p, t_p, DSA_KVH, DSA_HD)),
            st("pdi", (nb_p, t_p, IDX_D)),
            st("pret", (nb_p, RET_H, RET_DK, RET_DV)),
            st("pek", (nb_p, ntok, MEM_H, MEM_HD)), st("pev", (nb_p, ntok, MEM_H, MEM_HD)),
            st("smk", (nb_s, t_s, MOBA_H, MOBA_HD)), st("smv", (nb_s, t_s, MOBA_H, MOBA_HD)),
            st("sdk", (nb_s, t_s, DSA_KVH, DSA_HD)), st("sdv", (nb_s, t_s, DSA_KVH, DSA_HD)),
            st("sdi", (nb_s, t_s, IDX_D)),
            st("sret", (nb_s, RET_H, RET_DK, RET_DV)))
```
